```python
import math
import jax, jax.numpy as jnp
from jax import lax
import numpy as np

D_MODEL = 1024
BATCH = 2
SEQ = 8192
DEPTH = 4
DEC_BATCH = 128
DEC_SEQ = 8
PAST_LEN = 8192
PAGE_SIZE = 128

N_A_LAYERS = DEPTH // 2
N_B_LAYERS = DEPTH - N_A_LAYERS
CHUNK = 128
D_SGU = 2 * D_MODEL
SGU_GROUPS = 8
SGU_GROUP_DIM = D_SGU // SGU_GROUPS
HEAD_DIM = 64
N_HEADS = D_MODEL // HEAD_DIM
N_KV_HEADS = max(1, N_HEADS // 8)
GROUP = N_HEADS // N_KV_HEADS
WINDOW = 128
PEER_HEADS = 8
N_KEYS = 128
N_EXPERTS = N_KEYS * N_KEYS
D_KEY = 256
PEER_TOPK = 16
PEER_BLOCK = 256
D_PLE = 256
DEEPNORM_ALPHA = (2.0 * DEPTH) ** 0.25
DEEPNORM_BETA = (8.0 * DEPTH) ** -0.25
LN_EPS = 1e-5

kernel_name = "yoco_sgu_swa_sink_peer_decoder_step"


def layer_norm(x, g, b):
    xf = x.astype(jnp.float32)
    mu = jnp.mean(xf, -1, keepdims=True)
    var = jnp.mean(jnp.square(xf - mu), -1, keepdims=True)
    y = (xf - mu) * lax.rsqrt(var + LN_EPS)
    return (y * g.astype(jnp.float32) + b.astype(jnp.float32)).astype(x.dtype)


def alibi_slopes():
    s = [2.0 ** (-8.0 * (h + 1) / N_HEADS) for h in range(N_HEADS)]
    return jnp.asarray(s, jnp.float32).reshape(N_KV_HEADS, GROUP)


def sgu_mixer(x, w_in, b_in, ln_g, ln_b, w_s, b_s, w_out, b_out):
    bsz, L, _ = x.shape
    z = jax.nn.gelu(x @ w_in + b_in)
    u, v = jnp.split(z, 2, axis=-1)
    v = layer_norm(v, ln_g, ln_b)
    c = min(L, CHUNK)
    n = L // c
    mask = jnp.tril(jnp.ones((c, c), bool))
    ws = jnp.where(mask, w_s[:, :c, :c], 0.0)
    vc = v.reshape(bsz, n, c, SGU_GROUPS, SGU_GROUP_DIM)
    mixed = jnp.einsum('gts,bnsgc->bntgc', ws, vc) + b_s[:, :c].T[None, None, :, :, None]
    gated = u * mixed.reshape(bsz, L, D_SGU)
    return gated @ w_out + b_out, v


def shared_kv(h, w_kv):
    bsz, L, _ = h.shape
    kv = (h @ w_kv).reshape(bsz, L, 2, N_KV_HEADS, HEAD_DIM)
    return kv[:, :, 0], kv[:, :, 1]


def sink_softmax(s, mask, sinks):
    s = jnp.where(mask, s, -jnp.inf)
    sink = sinks.astype(jnp.float32)[:, :, None, None]
    m = jnp.maximum(jnp.max(s, -1, keepdims=True), sink)
    p = jnp.exp(s - m)
    return p / (jnp.sum(p, -1, keepdims=True) + jnp.exp(sink - m))


def window_attn_prompt(x, k, v, w_q, sinks, w_o, slopes):
    bsz, S, _ = x.shape
    nb = S // WINDOW
    q = (x @ w_q).reshape(bsz, nb, WINDOW, N_KV_HEADS, GROUP, HEAD_DIM)
    kb = k.reshape(bsz, nb, WINDOW, N_KV_HEADS, HEAD_DIM)
    vb = v.reshape(bsz, nb, WINDOW, N_KV_HEADS, HEAD_DIM)
    pad = ((0, 0), (1, 0), (0, 0), (0, 0), (0, 0))
    kk = jnp.concatenate([jnp.pad(kb, pad)[:, :-1], kb], axis=2)
    vv = jnp.concatenate([jnp.pad(vb, pad)[:, :-1], vb], axis=2)
    s = jnp.einsum('bnqkgd,bnskd->bnkgqs', q, kk).astype(jnp.float32) * (HEAD_DIM ** -0.5)
    qi = jnp.arange(WINDOW)[:, None] + WINDOW
    kj = jnp.arange(2 * WINDOW)[None, :]
    dist = qi - kj
    blk = jnp.arange(nb)[:, None, None]
    valid = (dist >= 0) & (dist < WINDOW) & ((blk > 0) | (kj >= WINDOW))
    s = s - slopes[:, :, None, None] * dist.astype(jnp.float32)
    p = sink_softmax(s, valid[None, :, None, None], sinks)
    o = jnp.einsum('bnkgqs,bnskd->bnqkgd', p.astype(vv.dtype), vv)
    return o.reshape(bsz, S, N_HEADS * HEAD_DIM) @ w_o


def window_attn_sample(x, k_all, v_all, w_q, sinks, w_o, slopes):
    bsz, L, _ = x.shape
    wc = k_all.shape[1] - L
    q = (x @ w_q).reshape(bsz, L, N_KV_HEADS, GROUP, HEAD_DIM)
    s = jnp.einsum('bqkgd,bskd->bkgqs', q, k_all).astype(jnp.float32) * (HEAD_DIM ** -0.5)
    dist = (jnp.arange(L)[:, None] + wc) - jnp.arange(wc + L)[None, :]
    valid = (dist >= 0) & (dist < WINDOW)
    s = s - slopes[:, :, None, None] * dist.astype(jnp.float32)
    p = sink_softmax(s, valid, sinks)
    o = jnp.einsum('bkgqs,bskd->bqkgd', p.astype(v_all.dtype), v_all)
    return o.reshape(bsz, L, N_HEADS * HEAD_DIM) @ w_o


def peer(x, w_q, subkeys, u_tab, v_tab):
    shp = x.shape
    xt = x.reshape(-1, D_MODEL)
    T = xt.shape[0]
    blk = min(PEER_BLOCK, T)
    Tp = -(-T // blk) * blk
    xb = jnp.pad(xt, ((0, Tp - T), (0, 0))).reshape(Tp // blk, blk, D_MODEL)

    def one_block(xs):
        q = (xs @ w_q).reshape(blk, PEER_HEADS, 2, D_KEY // 2)
        s = jnp.einsum('thpc,pnc->thpn', q, subkeys).astype(jnp.float32)
        s1, i1 = lax.top_k(s[:, :, 0], PEER_TOPK)
        s2, i2 = lax.top_k(s[:, :, 1], PEER_TOPK)
        cand = (s1[..., :, None] + s2[..., None, :]).reshape(blk, PEER_HEADS, PEER_TOPK * PEER_TOPK)
        cidx = (i1[..., :, None] * N_KEYS + i2[..., None, :]).reshape(blk, PEER_HEADS, PEER_TOPK * PEER_TOPK)
        top_s, pos = lax.top_k(cand, PEER_TOPK)
        eidx = jnp.take_along_axis(cidx, pos, -1).reshape(blk, PEER_HEADS * PEER_TOPK)
        g = jax.nn.softmax(top_s, -1).reshape(blk, PEER_HEADS * PEER_TOPK)
        act = jax.nn.gelu(jnp.einsum('ted,td->te', u_tab[eidx], xs))
        return jnp.einsum('te,ted->td', g.astype(act.dtype) * act, v_tab[eidx])

    y = lax.map(one_block, xb)
    return y.reshape(Tp, D_MODEL)[:T].reshape(shp)


def setup_inputs(seed: int = 0) -> dict:
    key = jax.random.key(seed)
    ks = iter(jax.random.split(key, 48))

    def nrm(shape, scale):
        return jax.random.normal(next(ks), shape, jnp.float32) * scale

    wc = min(WINDOW, PAST_LEN)
    kvd = N_KV_HEADS * HEAD_DIM
    return {
        "x_prompt": nrm((BATCH, SEQ, D_MODEL), 1.0),
        "x_sample": nrm((DEC_BATCH, DEC_SEQ, D_MODEL), 1.0),
        "cache_k_win": nrm((DEC_BATCH, wc, N_KV_HEADS, HEAD_DIM), 1.0),
        "cache_v_win": nrm((DEC_BATCH, wc, N_KV_HEADS, HEAD_DIM), 1.0),
        "p_prompt": nrm((DEPTH, BATCH, SEQ, D_PLE), 1.0),
        "p_sample": nrm((DEPTH, DEC_BATCH, DEC_SEQ, D_PLE), 1.0),
        "ln1_g": 1.0 + nrm((DEPTH, D_MODEL), 0.02),
        "ln1_b": nrm((DEPTH, D_MODEL), 0.02),
        "ln2_g": 1.0 + nrm((DEPTH, D_MODEL), 0.02),
        "ln2_b": nrm((DEPTH, D_MODEL), 0.02),
        "sgu_w_in": nrm((N_A_LAYERS, D_MODEL, 2 * D_SGU), D_MODEL ** -0.5),
        "sgu_b_in": nrm((N_A_LAYERS, 2 * D_SGU), 0.02),
        "sgu_ln_g": 1.0 + nrm((N_A_LAYERS, D_SGU), 0.02),
        "sgu_ln_b": nrm((N_A_LAYERS, D_SGU), 0.02),
        "sgu_w_s": nrm((N_A_LAYERS, SGU_GROUPS, CHUNK, CHUNK), 0.5 * CHUNK ** -0.5),
        "sgu_b_s": 1.0 + nrm((N_A_LAYERS, SGU_GROUPS, CHUNK), 0.02),
        "sgu_w_out": nrm((N_A_LAYERS, D_SGU, D_MODEL), DEEPNORM_BETA * D_SGU ** -0.5),
        "sgu_b_out": nrm((N_A_LAYERS, D_MODEL), 0.02),
        "attn_w_kv": nrm((D_MODEL, 2 * kvd), D_MODEL ** -0.5),
        "attn_w_q": nrm((N_B_LAYERS, D_MODEL, N_HEADS * HEAD_DIM), D_MODEL ** -0.5),
        "attn_sinks": nrm((N_B_LAYERS, N_HEADS), 0.5),
        "attn_w_o": nrm((N_B_LAYERS, N_HEADS * HEAD_DIM, D_MODEL), DEEPNORM_BETA * (N_HEADS * HEAD_DIM) ** -0.5),
        "peer_w_q": nrm((DEPTH, D_MODEL, PEER_HEADS * D_KEY), D_MODEL ** -0.5),
        "peer_subkeys": nrm((DEPTH, 2, N_KEYS, D_KEY // 2), (D_KEY // 2) ** -0.5),
        "peer_u": nrm((DEPTH, N_EXPERTS, D_MODEL), D_MODEL ** -0.5),
        "peer_v": nrm((DEPTH, N_EXPERTS, D_MODEL), DEEPNORM_BETA * PEER_HEADS ** -0.5),
        "ple_w": nrm((DEPTH, D_PLE, D_MODEL), D_PLE ** -0.5),
        "ple_gate_w": nrm((DEPTH, D_MODEL, D_MODEL), D_MODEL ** -0.5),
        "ple_gate_b": nrm((DEPTH, D_MODEL), 0.02),
    }


def reference(x_prompt, x_sample, cache_k_win, cache_v_win, p_prompt, p_sample,
              ln1_g, ln1_b, ln2_g, ln2_b,
              sgu_w_in, sgu_b_in, sgu_ln_g, sgu_ln_b, sgu_w_s, sgu_b_s, sgu_w_out, sgu_b_out,
              attn_w_kv, attn_w_q, attn_sinks, attn_w_o,
              peer_w_q, peer_subkeys, peer_u, peer_v,
              ple_w, ple_gate_w, ple_gate_b):
    slopes = alibi_slopes()

    def channel_and_ple(h, p, i):
        h = layer_norm(DEEPNORM_ALPHA * h + peer(h, peer_w_q[i], peer_subkeys[i], peer_u[i], peer_v[i]),
                       ln2_g[i], ln2_b[i])
        gate = jax.nn.sigmoid(h @ ple_gate_w[i] + ple_gate_b[i])
        return h + gate * (p @ ple_w[i])

    hp, hs = x_prompt, x_sample
    sgu_rows = []
    kp = vp = k_all = v_all = None
    for i in range(DEPTH):
        if i < N_A_LAYERS:
            a_args = (sgu_w_in[i], sgu_b_in[i], sgu_ln_g[i], sgu_ln_b[i],
                      sgu_w_s[i], sgu_b_s[i], sgu_w_out[i], sgu_b_out[i])
            mp, _ = sgu_mixer(hp, *a_args)
            ms, v_rows = sgu_mixer(hs, *a_args)
            sgu_rows.append(v_rows)
        else:
            if i == N_A_LAYERS:
                kp, vp = shared_kv(hp, attn_w_kv)
                kn, vn = shared_kv(hs, attn_w_kv)
                k_all = jnp.concatenate([cache_k_win.astype(kn.dtype), kn], axis=1)
                v_all = jnp.concatenate([cache_v_win.astype(vn.dtype), vn], axis=1)
            j = i - N_A_LAYERS
            sinks = attn_sinks[j].reshape(N_KV_HEADS, GROUP)
            mp = window_attn_prompt(hp, kp, vp, attn_w_q[j], sinks, attn_w_o[j], slopes)
            ms = window_attn_sample(hs, k_all, v_all, attn_w_q[j], sinks, attn_w_o[j], slopes)
        hp = layer_norm(DEEPNORM_ALPHA * hp + mp, ln1_g[i], ln1_b[i])
        hs = layer_norm(DEEPNORM_ALPHA * hs + ms, ln1_g[i], ln1_b[i])
        hp = channel_and_ple(hp, p_prompt[i], i)
        hs = channel_and_ple(hs, p_sample[i], i)

    wc = cache_k_win.shape[1]
    new_k_win_prompt = kp[:, -WINDOW:]
    new_v_win_prompt = vp[:, -WINDOW:]
    new_k_win_sample = k_all[:, -wc:]
    new_v_win_sample = v_all[:, -wc:]
    new_sgu_v_sample = jnp.stack(sgu_rows, axis=0)
    return (hp, hs, new_k_win_prompt, new_v_win_prompt, new_k_win_sample, new_v_win_sample, new_sgu_v_sample)
```

```python
import functools
import math

import jax
import jax.numpy as jnp
from jax import lax
from jax.experimental import pallas as pl
from jax.experimental.pallas import tpu as pltpu

D_MODEL = 1024
DEPTH = 4
N_A_LAYERS = DEPTH // 2
CHUNK = 128
D_SGU = 2 * D_MODEL
SGU_GROUPS = 8
SGU_GROUP_DIM = D_SGU // SGU_GROUPS
HEAD_DIM = 64
N_HEADS = D_MODEL // HEAD_DIM
N_KV_HEADS = 2
GROUP = N_HEADS // N_KV_HEADS
WINDOW = 128
PEER_HEADS = 8
N_KEYS = 128
N_EXPERTS = N_KEYS * N_KEYS
D_KEY = 256
PEER_TOPK = 16
D_PLE = 256
DEEPNORM_ALPHA = (2.0 * DEPTH) ** 0.25
LN_EPS = 1e-5

LANES = 128
VMEM_LIMIT_BYTES = 56 * 1024 * 1024

SGU_TOKENS = 256
PEER_TOKENS = 512
PEER_EXPERTS = 1024
POST_TOKENS = 512
KV_TOKENS = 512

NEG_INF = float("-inf")
NT_DIMS = (((1,), (1,)), ((), ()))

BF16 = jnp.bfloat16
F32 = jnp.float32


def _layer_norm(x, g, b):
    mu = jnp.mean(x, axis=-1, keepdims=True)
    xc = x - mu
    var = jnp.mean(xc * xc, axis=-1, keepdims=True)
    return xc * lax.rsqrt(var + LN_EPS) * g + b


def _dot(a, b):
    return jnp.dot(a, b, preferred_element_type=F32)


def _dot_nt(a, b):
    return lax.dot_general(a, b, NT_DIMS, preferred_element_type=F32)


def _const_spec(shape):
    n = len(shape)
    return pl.BlockSpec(shape, lambda *_: (0,) * n)


def _params(*semantics):
    return pltpu.CompilerParams(dimension_semantics=semantics,
                                vmem_limit_bytes=VMEM_LIMIT_BYTES)


def _sgu_kernel(x_ref, win_ref, bin_ref, lng_ref, lnb_ref, ws_ref, bs_ref,
                wout_ref, bout_ref, g1_ref, b1_ref, h_ref, v_ref,
                u_scr, v_scr):
    x = x_ref[...]
    xb = x.astype(BF16)
    n_col = (2 * D_SGU) // 512
    for j in range(n_col):
        z = _dot(xb, win_ref[:, j * 512:(j + 1) * 512]) + bin_ref[:, j * 512:(j + 1) * 512]
        z = jax.nn.gelu(z)
        if j < n_col // 2:
            u_scr[:, j * 512:(j + 1) * 512] = z
        else:
            jj = j - n_col // 2
            v_scr[:, jj * 512:(jj + 1) * 512] = z
    v = _layer_norm(v_scr[...], lng_ref[...], lnb_ref[...])
    v_ref[...] = v
    v_scr[...] = v
    for c in range(SGU_TOKENS // CHUNK):
        rows = slice(c * CHUNK, (c + 1) * CHUNK)
        for g in range(SGU_GROUPS):
            cols = slice(g * SGU_GROUP_DIM, (g + 1) * SGU_GROUP_DIM)
            mixed = _dot(ws_ref[0, g], v_scr[rows, cols].astype(BF16)) + bs_ref[0, :, cols]
            u_scr[rows, cols] = u_scr[rows, cols] * mixed
    out = _dot(u_scr[...].astype(BF16), wout_ref[...]) + bout_ref[...]
    h_ref[...] = _layer_norm(DEEPNORM_ALPHA * x + out, g1_ref[...], b1_ref[...])


def _sgu_layer(h, n_prompt_tokens, win, b_in, ln_g, ln_b, ws2, bs2, wout, b_out, g1, b1):
    t = h.shape[0]
    n_tiles = t // SGU_TOKENS
    n_prompt_tiles = n_prompt_tokens // SGU_TOKENS
    t_sample = t - n_prompt_tokens

    def kind(i):
        return jnp.where(i < n_prompt_tiles, 0, 1)

    return pl.pallas_call(
        _sgu_kernel,
        grid=(n_tiles,),
        in_specs=[
            pl.BlockSpec((SGU_TOKENS, D_MODEL), lambda i: (i, 0)),
            _const_spec((D_MODEL, 2 * D_SGU)),
            _const_spec((1, 2 * D_SGU)),
            _const_spec((1, D_SGU)),
            _const_spec((1, D_SGU)),
            pl.BlockSpec((1, SGU_GROUPS, CHUNK, CHUNK), lambda i: (kind(i), 0, 0, 0)),
            pl.BlockSpec((1, CHUNK, D_SGU), lambda i: (kind(i), 0, 0)),
            _const_spec((D_SGU, D_MODEL)),
            _const_spec((1, D_MODEL)),
            _const_spec((1, D_MODEL)),
            _const_spec((1, D_MODEL)),
        ],
        out_specs=[
            pl.BlockSpec((SGU_TOKENS, D_MODEL), lambda i: (i, 0)),
            pl.BlockSpec((SGU_TOKENS, D_SGU), lambda i: (jnp.maximum(i - n_prompt_tiles, 0), 0)),
        ],
        out_shape=[
            jax.ShapeDtypeStruct((t, D_MODEL), F32),
            jax.ShapeDtypeStruct((t_sample, D_SGU), F32),
        ],
        scratch_shapes=[
            pltpu.VMEM((SGU_TOKENS, D_SGU), F32),
            pltpu.VMEM((SGU_TOKENS, D_SGU), F32),
        ],
        compiler_params=_params("arbitrary"),
        name="sgu_layer",
    )(h, win, b_in, ln_g, ln_b, ws2, bs2, wout, b_out, g1, b1)


def _kv_kernel(h_ref, w_ref, k_ref, v_ref):
    kv = _dot(h_ref[...].astype(BF16), w_ref[...])
    half = N_KV_HEADS * HEAD_DIM
    k_ref[...] = kv[:, :half]
    v_ref[...] = kv[:, half:]


def _kv_proj(h, w_kv):
    t = h.shape[0]
    half = N_KV_HEADS * HEAD_DIM
    return pl.pallas_call(
        _kv_kernel,
        grid=(t // KV_TOKENS,),
        in_specs=[
            pl.BlockSpec((KV_TOKENS, D_MODEL), lambda i: (i, 0)),
            _const_spec((D_MODEL, 2 * half)),
        ],
        out_specs=[
            pl.BlockSpec((KV_TOKENS, half), lambda i: (i, 0)),
            pl.BlockSpec((KV_TOKENS, half), lambda i: (i, 0)),
        ],
        out_shape=[jax.ShapeDtypeStruct((t, half), F32)] * 2,
        compiler_params=_params("arbitrary"),
        name="kv_proj",
    )(h, w_kv)


def _attn_kernel(sink_ref, h_ref, kp_ref, kc_ref, vp_ref, vc_ref, wqT_ref, woT_ref,
                 g1_ref, b1_ref, o_ref, oT_scr, *, blocks_per_seq):
    hb = h_ref[...]
    qT = _dot_nt(wqT_ref[...], hb.astype(BF16)).astype(BF16)
    kk = jnp.concatenate([kp_ref[...], kc_ref[...]], axis=0).astype(BF16)
    vv = jnp.concatenate([vp_ref[...], vc_ref[...]], axis=0)
    vvT = vv.T.astype(BF16)

    kj = lax.broadcasted_iota(jnp.int32, (2 * WINDOW, WINDOW), 0)
    qi = lax.broadcasted_iota(jnp.int32, (2 * WINDOW, WINDOW), 1)
    dist = qi + WINDOW - kj
    valid = (dist >= 0) & (dist < WINDOW)
    if blocks_per_seq:
        is_first = (pl.program_id(0) % blocks_per_seq == 0).astype(jnp.int32)
        valid = valid & (kj >= is_first * WINDOW)
    distf = dist.astype(F32)

    zeros_q = jnp.zeros((HEAD_DIM, WINDOW), BF16)
    for h in range(N_HEADS):
        k = h // GROUP
        q_h = qT[h * HEAD_DIM:(h + 1) * HEAD_DIM]
        q_pad = jnp.concatenate([q_h, zeros_q] if k == 0 else [zeros_q, q_h], axis=0)
        s = _dot(kk, q_pad) * (HEAD_DIM ** -0.5)
        slope = 2.0 ** (-8.0 * (h + 1) / N_HEADS)
        s = s - slope * distf
        s = jnp.where(valid, s, NEG_INF)
        sink = sink_ref[h]
        m = jnp.maximum(jnp.max(s, axis=0, keepdims=True), sink)
        p = jnp.exp(s - m)
        den = jnp.sum(p, axis=0, keepdims=True) + jnp.exp(sink - m)
        p = p / den
        oT_scr[h * HEAD_DIM:(h + 1) * HEAD_DIM, :] = _dot(
            vvT[k * HEAD_DIM:(k + 1) * HEAD_DIM], p.astype(BF16))
    outT = _dot(woT_ref[...], oT_scr[...].astype(BF16))
    o_ref[...] = _layer_norm(DEEPNORM_ALPHA * hb + outT.T, g1_ref[...], b1_ref[...])


def _attn_layer(h, k_prev_arr, k_cur_arr, v_prev_arr, v_cur_arr, sinks, wqT, woT, g1, b1,
                *, blocks_per_seq):
    t = h.shape[0]
    n_blocks = t // WINDOW
    half = N_KV_HEADS * HEAD_DIM
    if blocks_per_seq:
        prev_map = lambda i: (jnp.maximum(i - 1, 0), 0)
    else:
        prev_map = lambda i: (i, 0)
    cur_map = lambda i: (i, 0)
    return pl.pallas_call(
        functools.partial(_attn_kernel, blocks_per_seq=blocks_per_seq),
        grid=(n_blocks,),
        in_specs=[
            pl.BlockSpec(memory_space=pltpu.SMEM),
            pl.BlockSpec((WINDOW, D_MODEL), cur_map),
            pl.BlockSpec((WINDOW, half), prev_map),
            pl.BlockSpec((WINDOW, half), cur_map),
            pl.BlockSpec((WINDOW, half), prev_map),
            pl.BlockSpec((WINDOW, half), cur_map),
            _const_spec((D_MODEL, D_MODEL)),
            _const_spec((D_MODEL, D_MODEL)),
            _const_spec((1, D_MODEL)),
            _const_spec((1, D_MODEL)),
        ],
        out_specs=pl.BlockSpec((WINDOW, D_MODEL), cur_map),
        out_shape=jax.ShapeDtypeStruct((t, D_MODEL), F32),
        scratch_shapes=[pltpu.VMEM((D_MODEL, WINDOW), F32)],
        compiler_params=_params("arbitrary"),
        name="attn_layer",
    )(sinks, h, k_prev_arr, k_cur_arr, v_prev_arr, v_cur_arr, wqT, woT, g1, b1)


def _top16_rows(s):
    rank = lax.broadcasted_iota(jnp.int32, (PEER_TOPK, LANES), 0)
    rows = []
    stacked = jnp.full((PEER_TOPK, LANES), NEG_INF, F32)
    cur = s
    for i in range(PEER_TOPK):
        m = jnp.max(cur, axis=0, keepdims=True)
        rows.append(m)
        stacked = jnp.where(rank == i, m, stacked)
        cur = jnp.where(cur == m, NEG_INF, cur)
    return rows, stacked


def _pair_threshold(r1, t1, r2, t2):
    row8 = lax.broadcasted_iota(jnp.int32, (8, LANES), 0)
    cands = []
    for i in range(8):
        n = PEER_TOPK // (i + 1)
        c = r1[i] + t2[0:8]
        if n < 8:
            c = jnp.where(row8 < n, c, NEG_INF)
        cands.append(c)
    cands.append(r1[0] + t2[8:16])
    cands.append(t1[8:16] + r2[0])
    top = r1[0] + r2[0]
    z = jnp.zeros((1, LANES), F32)
    m = top
    for _ in range(PEER_TOPK):
        m = cands[0]
        for c in cands[1:]:
            m = jnp.maximum(m, c)
        m = jnp.max(m, axis=0, keepdims=True)
        z = z + jnp.exp(m - top)
        cands = [jnp.where(c == m, NEG_INF, c) for c in cands]
    return m, z


def _peer_kernel(x_ref, wqT_ref, keys_ref, u_ref, vT_ref, o_ref,
                 xb_scr, s1_scr, s2_scr, e1_scr, e2_scr, tau_scr, hid_scr, g_scr, acc_scr):
    j = pl.program_id(1)
    n_lane_chunks = PEER_TOKENS // LANES
    a_per_step = PEER_EXPERTS // N_KEYS

    @pl.when(j == 0)
    def _select():
        xb_scr[...] = x_ref[...].astype(BF16)
        for h in range(PEER_HEADS):
            for p, dst in ((0, s1_scr), (1, s2_scr)):
                r0 = (h * 2 + p) * (D_KEY // 2)
                qT = _dot_nt(wqT_ref[r0:r0 + D_KEY // 2, :], xb_scr[...])
                dst[h] = _dot(keys_ref[p], qT.astype(BF16))

        def chunk_body(c, carry):
            lanes = pl.ds(pl.multiple_of(c * LANES, LANES), LANES)

            def head_body(h, carry2):
                s1 = s1_scr[h, :, lanes]
                s2 = s2_scr[h, :, lanes]
                r1, t1 = _top16_rows(s1)
                r2, t2 = _top16_rows(s2)
                tau, z = _pair_threshold(r1, t1, r2, t2)
                e1_scr[h, :, lanes] = jnp.exp(s1 - r1[0]) * (1.0 / z)
                e2_scr[h, :, lanes] = jnp.exp(s2 - r2[0])
                tau_scr[h, :, lanes] = jnp.broadcast_to(tau, (8, LANES))
                return carry2

            return lax.fori_loop(0, PEER_HEADS, head_body, carry)

        lax.fori_loop(0, n_lane_chunks, chunk_body, 0)
        acc_scr[...] = jnp.zeros_like(acc_scr)

    hid_scr[...] = _dot_nt(u_ref[...], xb_scr[...])

    def dense_body(c, carry):
        lanes = pl.ds(pl.multiple_of(c * LANES, LANES), LANES)
        a_rows = pl.ds(pl.multiple_of(j * a_per_step, a_per_step), a_per_step)
        s1_blk = [s1_scr[h, a_rows, lanes] for h in range(PEER_HEADS)]
        e1_blk = [e1_scr[h, a_rows, lanes] for h in range(PEER_HEADS)]
        for al in range(a_per_step):
            w = jnp.zeros((N_KEYS, LANES), F32)
            for h in range(PEER_HEADS):
                s1row = s1_blk[h][al:al + 1]
                e1row = e1_blk[h][al:al + 1]
                tau = tau_scr[h, 0:1, lanes]
                pair = s1row + s2_scr[h, :, lanes]
                w = w + jnp.where(pair >= tau, e1row * e2_scr[h, :, lanes], 0.0)
            rows = slice(al * N_KEYS, (al + 1) * N_KEYS)
            g_scr[rows, lanes] = (jax.nn.gelu(hid_scr[rows, lanes]) * w).astype(BF16)
        return carry

    lax.fori_loop(0, n_lane_chunks, dense_body, 0)
    acc_scr[...] += _dot(vT_ref[...], g_scr[...])

    @pl.when(j == pl.num_programs(1) - 1)
    def _finish():
        o_ref[...] = acc_scr[...].T


def _peer_layer(h, wqT, keys, u_tab, vT_tab):
    t = h.shape[0]
    return pl.pallas_call(
        _peer_kernel,
        grid=(t // PEER_TOKENS, N_EXPERTS // PEER_EXPERTS),
        in_specs=[
            pl.BlockSpec((PEER_TOKENS, D_MODEL), lambda i, j: (i, 0)),
            _const_spec((PEER_HEADS * D_KEY, D_MODEL)),
            _const_spec((2, N_KEYS, D_KEY // 2)),
            pl.BlockSpec((PEER_EXPERTS, D_MODEL), lambda i, j: (j, 0)),
            pl.BlockSpec((D_MODEL, PEER_EXPERTS), lambda i, j: (0, j)),
        ],
        out_specs=pl.BlockSpec((PEER_TOKENS, D_MODEL), lambda i, j: (i, 0)),
        out_shape=jax.ShapeDtypeStruct((t, D_MODEL), F32),
        scratch_shapes=[
            pltpu.VMEM((PEER_TOKENS, D_MODEL), BF16),
            pltpu.VMEM((PEER_HEADS, N_KEYS, PEER_TOKENS), F32),
            pltpu.VMEM((PEER_HEADS, N_KEYS, PEER_TOKENS), F32),
            pltpu.VMEM((PEER_HEADS, N_KEYS, PEER_TOKENS), F32),
            pltpu.VMEM((PEER_HEADS, N_KEYS, PEER_TOKENS), F32),
            pltpu.VMEM((PEER_HEADS, 8, PEER_TOKENS), F32),
            pltpu.VMEM((PEER_EXPERTS, PEER_TOKENS), F32),
            pltpu.VMEM((PEER_EXPERTS, PEER_TOKENS), BF16),
            pltpu.VMEM((D_MODEL, PEER_TOKENS), F32),
        ],
        compiler_params=_params("arbitrary", "arbitrary"),
        name="peer_layer",
    )(h, wqT, keys, u_tab, vT_tab)


def _post_kernel(h_ref, y_ref, p_ref, g2_ref, b2_ref, wg_ref, bg_ref, wp_ref, o_ref):
    hn = _layer_norm(DEEPNORM_ALPHA * h_ref[...] + y_ref[...], g2_ref[...], b2_ref[...])
    gate = jax.nn.sigmoid(_dot(hn.astype(BF16), wg_ref[...]) + bg_ref[...])
    o_ref[...] = hn + gate * _dot(p_ref[...].astype(BF16), wp_ref[...])


def _post_layer(h, y, p, g2, b2, wg, bg, wp):
    t = h.shape[0]
    tok = lambda i: (i, 0)
    return pl.pallas_call(
        _post_kernel,
        grid=(t // POST_TOKENS,),
        in_specs=[
            pl.BlockSpec((POST_TOKENS, D_MODEL), tok),
            pl.BlockSpec((POST_TOKENS, D_MODEL), tok),
            pl.BlockSpec((POST_TOKENS, D_PLE), tok),
            _const_spec((1, D_MODEL)),
            _const_spec((1, D_MODEL)),
            _const_spec((D_MODEL, D_MODEL)),
            _const_spec((1, D_MODEL)),
            _const_spec((D_PLE, D_MODEL)),
        ],
        out_specs=pl.BlockSpec((POST_TOKENS, D_MODEL), tok),
        out_shape=jax.ShapeDtypeStruct((t, D_MODEL), F32),
        compiler_params=_params("arbitrary"),
        name="post_layer",
    )(h, y, p, g2, b2, wg, bg, wp)


def _row(v):
    return v.reshape(1, -1).astype(F32)


def _sgu_spatial_params(w_s, b_s, dec_seq):
    tri = jnp.tril(jnp.ones((CHUNK, CHUNK), bool))
    ws_prompt = jnp.where(tri, w_s, 0.0)
    tri_s = jnp.tril(jnp.ones((dec_seq, dec_seq), bool))
    small = jnp.where(tri_s, w_s[:, :dec_seq, :dec_seq], 0.0)
    eye = jnp.eye(CHUNK // dec_seq, dtype=w_s.dtype)
    ws_sample = jnp.einsum("ab,gts->gatbs", eye, small).reshape(SGU_GROUPS, CHUNK, CHUNK)
    ws2 = jnp.stack([ws_prompt, ws_sample]).astype(BF16)
    bs_prompt = b_s.T
    bs_sample = jnp.tile(b_s[:, :dec_seq].T, (CHUNK // dec_seq, 1))
    bs2 = jnp.stack([bs_prompt, bs_sample])
    bs2 = jnp.repeat(bs2, SGU_GROUP_DIM, axis=2).astype(F32)
    return ws2, bs2


def kernel(x_prompt, x_sample, cache_k_win, cache_v_win, p_prompt, p_sample,
           ln1_g, ln1_b, ln2_g, ln2_b,
           sgu_w_in, sgu_b_in, sgu_ln_g, sgu_ln_b, sgu_w_s, sgu_b_s, sgu_w_out, sgu_b_out,
           attn_w_kv, attn_w_q, attn_sinks, attn_w_o,
           peer_w_q, peer_subkeys, peer_u, peer_v,
           ple_w, ple_gate_w, ple_gate_b):
    batch, seq, _ = x_prompt.shape
    dec_batch, dec_seq, _ = x_sample.shape
    tp = batch * seq
    ts = dec_batch * dec_seq
    half = N_KV_HEADS * HEAD_DIM
    wc = cache_k_win.shape[1]
    assert wc == WINDOW and CHUNK % dec_seq == 0 and seq % WINDOW == 0
    assert tp % PEER_TOKENS == 0 and ts % PEER_TOKENS == 0 and ts % SGU_TOKENS == 0

    h = jnp.concatenate([x_prompt.reshape(tp, D_MODEL), x_sample.reshape(ts, D_MODEL)], axis=0)
    p_all = jnp.concatenate([p_prompt.reshape(DEPTH, tp, D_PLE),
                             p_sample.reshape(DEPTH, ts, D_PLE)], axis=1)

    sgu_rows = []
    k_all = v_all = None
    for i in range(DEPTH):
        g1, b1 = _row(ln1_g[i]), _row(ln1_b[i])
        if i < N_A_LAYERS:
            ws2, bs2 = _sgu_spatial_params(sgu_w_s[i], sgu_b_s[i], dec_seq)
            h, v_rows = _sgu_layer(
                h, tp, sgu_w_in[i].astype(BF16), _row(sgu_b_in[i]), _row(sgu_ln_g[i]),
                _row(sgu_ln_b[i]), ws2, bs2, sgu_w_out[i].astype(BF16), _row(sgu_b_out[i]), g1, b1)
            sgu_rows.append(v_rows.reshape(dec_batch, dec_seq, D_SGU))
        else:
            if i == N_A_LAYERS:
                k_all, v_all = _kv_proj(h, attn_w_kv.astype(BF16))
                kp, vp = k_all[:tp], v_all[:tp]
                kn = k_all[tp:].reshape(dec_batch, dec_seq, half)
                vn = v_all[tp:].reshape(dec_batch, dec_seq, half)
                pad = ((0, 0), (0, WINDOW - dec_seq), (0, 0))
                kn_pad = jnp.pad(kn, pad).reshape(dec_batch * WINDOW, half)
                vn_pad = jnp.pad(vn, pad).reshape(dec_batch * WINDOW, half)
                ck = cache_k_win.reshape(dec_batch * wc, half).astype(F32)
                cv = cache_v_win.reshape(dec_batch * wc, half).astype(F32)
            jl = i - N_A_LAYERS
            wqT = attn_w_q[jl].T.astype(BF16)
            woT = attn_w_o[jl].T.astype(BF16)
            sinks = attn_sinks[jl].astype(F32)
            hp = _attn_layer(h[:tp], kp, kp, vp, vp, sinks, wqT, woT, g1, b1,
                             blocks_per_seq=seq // WINDOW)
            hs_pad = jnp.pad(h[tp:].reshape(dec_batch, dec_seq, D_MODEL),
                             ((0, 0), (0, WINDOW - dec_seq), (0, 0))).reshape(dec_batch * WINDOW, D_MODEL)
            hs = _attn_layer(hs_pad, ck, kn_pad, cv, vn_pad, sinks, wqT, woT, g1, b1,
                             blocks_per_seq=0)
            hs = hs.reshape(dec_batch, WINDOW, D_MODEL)[:, :dec_seq].reshape(ts, D_MODEL)
            h = jnp.concatenate([hp, hs], axis=0)
        y = _peer_layer(h, peer_w_q[i].T.astype(BF16), peer_subkeys[i].astype(BF16),
                        peer_u[i].astype(BF16), peer_v[i].T.astype(BF16))
        h = _post_layer(h, y, p_all[i], _row(ln2_g[i]), _row(ln2_b[i]),
                        ple_gate_w[i].astype(BF16), _row(ple_gate_b[i]), ple_w[i].astype(BF16))

    y_prompt = h[:tp].reshape(batch, seq, D_MODEL)
    y_sample = h[tp:].reshape(dec_batch, dec_seq, D_MODEL)
    kp4 = kp.reshape(batch, seq, N_KV_HEADS, HEAD_DIM)
    vp4 = vp.reshape(batch, seq, N_KV_HEADS, HEAD_DIM)
    new_k_win_prompt = kp4[:, -WINDOW:]
    new_v_win_prompt = vp4[:, -WINDOW:]
    kn4 = kn.reshape(dec_batch, dec_seq, N_KV_HEADS, HEAD_DIM)
    vn4 = vn.reshape(dec_batch, dec_seq, N_KV_HEADS, HEAD_DIM)
    new_k_win_sample = jnp.concatenate([cache_k_win.astype(F32), kn4], axis=1)[:, -wc:]
    new_v_win_sample = jnp.concatenate([cache_v_win.astype(F32), vn4], axis=1)[:, -wc:]
    new_sgu_v_sample = jnp.stack(sgu_rows, axis=0)
    return (y_prompt, y_sample, new_k_win_prompt, new_v_win_prompt,
            new_k_win_sample, new_v_win_sample, new_sgu_v_sample)
```

```python
import functools
import math

import jax
import jax.numpy as jnp
from jax import lax
from jax.experimental import pallas as pl
from jax.experimental.pallas import tpu as pltpu

D_MODEL = 1024
DEPTH = 4
N_A_LAYERS = DEPTH // 2
CHUNK = 128
D_SGU = 2 * D_MODEL
SGU_GROUPS = 8
SGU_GROUP_DIM = D_SGU // SGU_GROUPS
HEAD_DIM = 64
N_HEADS = D_MODEL // HEAD_DIM
N_KV_HEADS = 2
GROUP = N_HEADS // N_KV_HEADS
WINDOW = 128
PEER_HEADS = 8
N_KEYS = 128
N_EXPERTS = N_KEYS * N_KEYS
D_KEY = 256
PEER_TOPK = 16
D_PLE = 256
DEEPNORM_ALPHA = (2.0 * DEPTH) ** 0.25
LN_EPS = 1e-5

LANES = 128
MXU_DIM = 256
VMEM_LIMIT_BYTES = 56 * 1024 * 1024

SGU_TOKENS = 256
PEER_TOKENS = 512
PEER_EXPERTS = 1024
POST_TOKENS = 512
KV_TOKENS = 512

NEG_INF = float("-inf")
NT_DIMS = (((1,), (1,)), ((), ()))

BF16 = jnp.bfloat16
F32 = jnp.float32


def _layer_norm(x, g, b):
    mu = jnp.mean(x, axis=-1, keepdims=True)
    xc = x - mu
    var = jnp.mean(xc * xc, axis=-1, keepdims=True)
    return xc * lax.rsqrt(var + LN_EPS) * g + b


def _dot(a, b):
    return jnp.dot(a, b, preferred_element_type=F32)


def _dot_nt(a, b):
    return lax.dot_general(a, b, NT_DIMS, preferred_element_type=F32)


def _const_spec(shape):
    n = len(shape)
    return pl.BlockSpec(shape, lambda *_: (0,) * n)


def _params(*semantics, flags=None):
    return pltpu.CompilerParams(dimension_semantics=semantics,
                                vmem_limit_bytes=VMEM_LIMIT_BYTES, flags=flags)


def _sgu_kernel(x_ref, win_ref, bin_ref, lng_ref, lnb_ref, ws_ref, bs_ref,
                wout_ref, bout_ref, g1_ref, b1_ref, h_ref, v_ref,
                u_scr, v_scr):
    x = x_ref[...]
    xb = x.astype(BF16)
    n_col = (2 * D_SGU) // 512
    for j in range(n_col):
        z = _dot(xb, win_ref[:, j * 512:(j + 1) * 512]) + bin_ref[:, j * 512:(j + 1) * 512]
        z = jax.nn.gelu(z)
        if j < n_col // 2:
            u_scr[:, j * 512:(j + 1) * 512] = z
        else:
            jj = j - n_col // 2
            v_scr[:, jj * 512:(jj + 1) * 512] = z
    v = _layer_norm(v_scr[...], lng_ref[...], lnb_ref[...])
    v_ref[...] = v
    v_scr[...] = v
    for c in range(SGU_TOKENS // CHUNK):
        rows = slice(c * CHUNK, (c + 1) * CHUNK)
        for g in range(SGU_GROUPS):
            cols = slice(g * SGU_GROUP_DIM, (g + 1) * SGU_GROUP_DIM)
            mixed = _dot(ws_ref[0, g], v_scr[rows, cols].astype(BF16)) + bs_ref[0, :, cols]
            u_scr[rows, cols] = u_scr[rows, cols] * mixed
    out = _dot(u_scr[...].astype(BF16), wout_ref[...]) + bout_ref[...]
    h_ref[...] = _layer_norm(DEEPNORM_ALPHA * x + out, g1_ref[...], b1_ref[...])


def _sgu_layer(h, n_prompt_tokens, win, b_in, ln_g, ln_b, ws2, bs2, wout, b_out, g1, b1):
    t = h.shape[0]
    n_tiles = t // SGU_TOKENS
    n_prompt_tiles = n_prompt_tokens // SGU_TOKENS
    t_sample = t - n_prompt_tokens

    def kind(i):
        return jnp.where(i < n_prompt_tiles, 0, 1)

    return pl.pallas_call(
        _sgu_kernel,
        grid=(n_tiles,),
        in_specs=[
            pl.BlockSpec((SGU_TOKENS, D_MODEL), lambda i: (i, 0)),
            _const_spec((D_MODEL, 2 * D_SGU)),
            _const_spec((1, 2 * D_SGU)),
            _const_spec((1, D_SGU)),
            _const_spec((1, D_SGU)),
            pl.BlockSpec((1, SGU_GROUPS, CHUNK, CHUNK), lambda i: (kind(i), 0, 0, 0)),
            pl.BlockSpec((1, CHUNK, D_SGU), lambda i: (kind(i), 0, 0)),
            _const_spec((D_SGU, D_MODEL)),
            _const_spec((1, D_MODEL)),
            _const_spec((1, D_MODEL)),
            _const_spec((1, D_MODEL)),
        ],
        out_specs=[
            pl.BlockSpec((SGU_TOKENS, D_MODEL), lambda i: (i, 0)),
            pl.BlockSpec((SGU_TOKENS, D_SGU), lambda i: (jnp.maximum(i - n_prompt_tiles, 0), 0)),
        ],
        out_shape=[
            jax.ShapeDtypeStruct((t, D_MODEL), F32),
            jax.ShapeDtypeStruct((t_sample, D_SGU), F32),
        ],
        scratch_shapes=[
            pltpu.VMEM((SGU_TOKENS, D_SGU), F32),
            pltpu.VMEM((SGU_TOKENS, D_SGU), F32),
        ],
        compiler_params=_params("arbitrary"),
        name="sgu_layer",
    )(h, win, b_in, ln_g, ln_b, ws2, bs2, wout, b_out, g1, b1)


def _kv_kernel(h_ref, w_ref, k_ref, v_ref):
    kv = _dot(h_ref[...].astype(BF16), w_ref[...])
    half = N_KV_HEADS * HEAD_DIM
    k_ref[...] = kv[:, :half]
    v_ref[...] = kv[:, half:]


def _kv_proj(h, w_kv):
    t = h.shape[0]
    half = N_KV_HEADS * HEAD_DIM
    return pl.pallas_call(
        _kv_kernel,
        grid=(t // KV_TOKENS,),
        in_specs=[
            pl.BlockSpec((KV_TOKENS, D_MODEL), lambda i: (i, 0)),
            _const_spec((D_MODEL, 2 * half)),
        ],
        out_specs=[
            pl.BlockSpec((KV_TOKENS, half), lambda i: (i, 0)),
            pl.BlockSpec((KV_TOKENS, half), lambda i: (i, 0)),
        ],
        out_shape=[jax.ShapeDtypeStruct((t, half), F32)] * 2,
        compiler_params=_params("arbitrary"),
        name="kv_proj",
    )(h, w_kv)


def _attn_kernel(sink_ref, h_ref, kp_ref, kc_ref, vp_ref, vc_ref, wqT_ref, woT_ref,
                 g1_ref, b1_ref, o_ref, oT_scr, *, blocks_per_seq):
    hb = h_ref[...]
    qT = _dot_nt(wqT_ref[...], hb.astype(BF16)).astype(BF16)
    kk = jnp.concatenate([kp_ref[...], kc_ref[...]], axis=0).astype(BF16)
    vv = jnp.concatenate([vp_ref[...], vc_ref[...]], axis=0)
    vvT = vv.T.astype(BF16)

    kj = lax.broadcasted_iota(jnp.int32, (2 * WINDOW, WINDOW), 0)
    qi = lax.broadcasted_iota(jnp.int32, (2 * WINDOW, WINDOW), 1)
    dist = qi + WINDOW - kj
    valid = (dist >= 0) & (dist < WINDOW)
    if blocks_per_seq:
        is_first = (pl.program_id(0) % blocks_per_seq == 0).astype(jnp.int32)
        valid = valid & (kj >= is_first * WINDOW)
    distf = dist.astype(F32)

    zeros_q = jnp.zeros((HEAD_DIM, GROUP * WINDOW), BF16)
    for k in range(N_KV_HEADS):
        heads = range(k * GROUP, (k + 1) * GROUP)
        q_grp = jnp.concatenate([qT[h * HEAD_DIM:(h + 1) * HEAD_DIM] for h in heads], axis=1)
        q_pad = jnp.concatenate([q_grp, zeros_q] if k == 0 else [zeros_q, q_grp], axis=0)
        s_all = _dot(kk, q_pad) * (HEAD_DIM ** -0.5)
        p_blocks = []
        for g, h in enumerate(heads):
            slope = 2.0 ** (-8.0 * (h + 1) / N_HEADS)
            s = s_all[:, g * WINDOW:(g + 1) * WINDOW] - slope * distf
            s = jnp.where(valid, s, NEG_INF)
            sink = sink_ref[h]
            m = jnp.maximum(jnp.max(s, axis=0, keepdims=True), sink)
            p = jnp.exp(s - m)
            den = jnp.sum(p, axis=0, keepdims=True) + jnp.exp(sink - m)
            p_blocks.append((p / den).astype(BF16))
        o_all = _dot(vvT[k * HEAD_DIM:(k + 1) * HEAD_DIM],
                     jnp.concatenate(p_blocks, axis=1))
        for g, h in enumerate(heads):
            oT_scr[h * HEAD_DIM:(h + 1) * HEAD_DIM, :] = o_all[:, g * WINDOW:(g + 1) * WINDOW]
    outT = _dot(woT_ref[...], oT_scr[...].astype(BF16))
    o_ref[...] = _layer_norm(DEEPNORM_ALPHA * hb + outT.T, g1_ref[...], b1_ref[...])


def _attn_layer(h, k_prev_arr, k_cur_arr, v_prev_arr, v_cur_arr, sinks, wqT, woT, g1, b1,
                *, blocks_per_seq):
    t = h.shape[0]
    n_blocks = t // WINDOW
    half = N_KV_HEADS * HEAD_DIM
    if blocks_per_seq:
        prev_map = lambda i: (jnp.maximum(i - 1, 0), 0)
    else:
        prev_map = lambda i: (i, 0)
    cur_map = lambda i: (i, 0)
    return pl.pallas_call(
        functools.partial(_attn_kernel, blocks_per_seq=blocks_per_seq),
        grid=(n_blocks,),
        in_specs=[
            pl.BlockSpec(memory_space=pltpu.SMEM),
            pl.BlockSpec((WINDOW, D_MODEL), cur_map),
            pl.BlockSpec((WINDOW, half), prev_map),
            pl.BlockSpec((WINDOW, half), cur_map),
            pl.BlockSpec((WINDOW, half), prev_map),
            pl.BlockSpec((WINDOW, half), cur_map),
            _const_spec((D_MODEL, D_MODEL)),
            _const_spec((D_MODEL, D_MODEL)),
            _const_spec((1, D_MODEL)),
            _const_spec((1, D_MODEL)),
        ],
        out_specs=pl.BlockSpec((WINDOW, D_MODEL), cur_map),
        out_shape=jax.ShapeDtypeStruct((t, D_MODEL), F32),
        scratch_shapes=[pltpu.VMEM((D_MODEL, WINDOW), F32)],
        compiler_params=_params("arbitrary"),
        name="attn_layer",
    )(sinks, h, k_prev_arr, k_cur_arr, v_prev_arr, v_cur_arr, wqT, woT, g1, b1)


NOT_RANKED = 127.0


def _top16_rows(s, want_rank):
    row_id = lax.broadcasted_iota(jnp.int32, (PEER_TOPK, LANES), 0)
    rows = []
    stacked = jnp.full((PEER_TOPK, LANES), NEG_INF, F32)
    rank = jnp.full(s.shape, NOT_RANKED, F32) if want_rank else None
    cur = s
    for i in range(PEER_TOPK):
        m = jnp.max(cur, axis=0, keepdims=True)
        rows.append(m)
        stacked = jnp.where(row_id == i, m, stacked)
        hit = cur == m
        if want_rank:
            rank = jnp.where(hit, float(i), rank)
        cur = jnp.where(hit, NEG_INF, cur)
    return rows, stacked, rank


def _pair_threshold(r1, t1, r2, t2):
    row8 = lax.broadcasted_iota(jnp.int32, (8, LANES), 0)
    cands = []
    for i in range(8):
        n = PEER_TOPK // (i + 1)
        c = r1[i] + t2[0:8]
        if n < 8:
            c = jnp.where(row8 < n, c, NEG_INF)
        cands.append(c)
    cands.append(r1[0] + t2[8:16])
    cands.append(t1[8:16] + r2[0])
    top = r1[0] + r2[0]
    z = jnp.zeros((1, LANES), F32)
    m = top
    for _ in range(PEER_TOPK):
        m = cands[0]
        for c in cands[1:]:
            m = jnp.maximum(m, c)
        m = jnp.max(m, axis=0, keepdims=True)
        z = z + jnp.exp(m - top)
        cands = [jnp.where(c == m, NEG_INF, c) for c in cands]
    return m, z


def _peer_select(x_ref, wqT_ref, keys_ref, xb_scr, qT_scr, s1_scr, s2_scr,
                 cnt_scr, e1_scr, rank2_scr, e2_scr):
    xb_scr[...] = x_ref[...].T.astype(BF16)
    qT_scr[...] = _dot(wqT_ref[...], xb_scr[...]).astype(BF16)
    for h in range(PEER_HEADS):
        for p, dst in ((0, s1_scr), (1, s2_scr)):
            r0 = (h * 2 + p) * (D_KEY // 2)
            dst[h] = _dot(keys_ref[p], qT_scr[r0:r0 + D_KEY // 2, :])

    def chunk_body(c, carry):
        lanes = pl.ds(pl.multiple_of(c * LANES, LANES), LANES)

        def head_body(h, carry2):
            s1 = s1_scr[h, :, lanes]
            s2 = s2_scr[h, :, lanes]
            r1, t1, _ = _top16_rows(s1, False)
            r2, t2, rank2 = _top16_rows(s2, True)
            tau, z = _pair_threshold(r1, t1, r2, t2)
            cnt = jnp.zeros(s1.shape, F32)
            for jx in range(PEER_TOPK):
                cnt = cnt + jnp.where(s1 + r2[jx] >= tau, 1.0, 0.0)
            cnt_scr[h, :, lanes] = cnt
            e1_scr[h, :, lanes] = jnp.exp(s1 - r1[0]) * (1.0 / z)
            rank2_scr[h, :, lanes] = rank2.astype(BF16)
            e2_scr[h, :, lanes] = jnp.exp(s2 - r2[0]).astype(BF16)
            return carry2

        return lax.fori_loop(0, PEER_HEADS, head_body, carry)

    lax.fori_loop(0, PEER_TOKENS // LANES, chunk_body, 0)


def _peer_step(j, xb_scr, cnt_scr, e1_scr, rank2_scr, e2_scr, u_ref, vT_ref,
               w_scr, acc_scr, g_write, g_read):
    a_per_step = PEER_EXPERTS // N_KEYS
    tile = jnp.minimum(j, N_EXPERTS // PEER_EXPERTS - 1)
    a_rows = pl.ds(pl.multiple_of(tile * a_per_step, a_per_step), a_per_step)
    zero = jnp.zeros((), BF16)

    for c in range(PEER_TOKENS // LANES):
        lanes = slice(c * LANES, (c + 1) * LANES)
        cnt_blk = [cnt_scr[h, a_rows, lanes] for h in range(PEER_HEADS)]
        e1_blk = [e1_scr[h, a_rows, lanes] for h in range(PEER_HEADS)]
        for al in range(a_per_step):
            w = None
            for h in range(PEER_HEADS):
                cnt = jnp.broadcast_to(cnt_blk[h][al:al + 1], (N_KEYS, LANES)).astype(BF16)
                e1 = jnp.broadcast_to(e1_blk[h][al:al + 1], (N_KEYS, LANES)).astype(BF16)
                wh = jnp.where(rank2_scr[h, :, lanes] < cnt, e2_scr[h, :, lanes] * e1, zero)
                w = wh if w is None else w + wh
            w_scr[al * N_KEYS:(al + 1) * N_KEYS, lanes] = w

    hid = _dot(u_ref[...], xb_scr[...])
    acc_scr[...] += _dot(vT_ref[...], g_read[...])
    g_write[...] = jax.nn.gelu(hid).astype(BF16) * w_scr[...]


def _peer_kernel(x_ref, wqT_ref, keys_ref, u_ref, vT_ref, o_ref,
                 xb_scr, qT_scr, s1_scr, s2_scr, cnt_scr, e1_scr, rank2_scr, e2_scr,
                 w_scr, g0_scr, g1_scr, acc_scr):
    j = pl.program_id(1)

    @pl.when(j == 0)
    def _start():
        _peer_select(x_ref, wqT_ref, keys_ref, xb_scr, qT_scr, s1_scr, s2_scr,
                     cnt_scr, e1_scr, rank2_scr, e2_scr)
        acc_scr[...] = jnp.zeros_like(acc_scr)
        g1_scr[...] = jnp.zeros_like(g1_scr)

    step = functools.partial(_peer_step, j, xb_scr, cnt_scr, e1_scr, rank2_scr, e2_scr,
                             u_ref, vT_ref, w_scr, acc_scr)

    @pl.when(j % 2 == 0)
    def _even():
        step(g0_scr, g1_scr)

    @pl.when(j % 2 == 1)
    def _odd():
        step(g1_scr, g0_scr)

    @pl.when(j == pl.num_programs(1) - 1)
    def _finish():
        o_ref[...] = acc_scr[...].T


def _peer_layer(h, wqT, keys, u_tab, vT_tab):
    t = h.shape[0]
    n_tiles = N_EXPERTS // PEER_EXPERTS
    assert n_tiles % 2 == 0
    return pl.pallas_call(
        _peer_kernel,
        grid=(t // PEER_TOKENS, n_tiles + 1),
        in_specs=[
            pl.BlockSpec((PEER_TOKENS, D_MODEL), lambda i, j: (i, 0)),
            _const_spec((PEER_HEADS * D_KEY, D_MODEL)),
            _const_spec((2, N_KEYS, D_KEY // 2)),
            pl.BlockSpec((PEER_EXPERTS, D_MODEL), lambda i, j: (jnp.minimum(j, n_tiles - 1), 0)),
            pl.BlockSpec((D_MODEL, PEER_EXPERTS), lambda i, j: (0, jnp.maximum(j - 1, 0))),
        ],
        out_specs=pl.BlockSpec((PEER_TOKENS, D_MODEL), lambda i, j: (i, 0)),
        out_shape=jax.ShapeDtypeStruct((t, D_MODEL), F32),
        scratch_shapes=[
            pltpu.VMEM((D_MODEL, PEER_TOKENS), BF16),
            pltpu.VMEM((PEER_HEADS * D_KEY, PEER_TOKENS), BF16),
            pltpu.VMEM((PEER_HEADS, N_KEYS, PEER_TOKENS), F32),
            pltpu.VMEM((PEER_HEADS, N_KEYS, PEER_TOKENS), F32),
            pltpu.VMEM((PEER_HEADS, N_KEYS, PEER_TOKENS), F32),
            pltpu.VMEM((PEER_HEADS, N_KEYS, PEER_TOKENS), F32),
            pltpu.VMEM((PEER_HEADS, N_KEYS, PEER_TOKENS), BF16),
            pltpu.VMEM((PEER_HEADS, N_KEYS, PEER_TOKENS), BF16),
            pltpu.VMEM((PEER_EXPERTS, PEER_TOKENS), BF16),
            pltpu.VMEM((PEER_EXPERTS, PEER_TOKENS), BF16),
            pltpu.VMEM((PEER_EXPERTS, PEER_TOKENS), BF16),
            pltpu.VMEM((D_MODEL, PEER_TOKENS), F32),
        ],
        compiler_params=_params("arbitrary", "arbitrary"),
        name="peer_layer",
    )(h, wqT, keys, u_tab, vT_tab)


def _post_kernel(h_ref, y_ref, p_ref, g2_ref, b2_ref, wg_ref, bg_ref, wp_ref, o_ref):
    hn = _layer_norm(DEEPNORM_ALPHA * h_ref[...] + y_ref[...], g2_ref[...], b2_ref[...])
    gate = jax.nn.sigmoid(_dot(hn.astype(BF16), wg_ref[...]) + bg_ref[...])
    o_ref[...] = hn + gate * _dot(p_ref[...].astype(BF16), wp_ref[...])


def _post_layer(h, y, p, g2, b2, wg, bg, wp):
    t = h.shape[0]
    tok = lambda i: (i, 0)
    return pl.pallas_call(
        _post_kernel,
        grid=(t // POST_TOKENS,),
        in_specs=[
            pl.BlockSpec((POST_TOKENS, D_MODEL), tok),
            pl.BlockSpec((POST_TOKENS, D_MODEL), tok),
            pl.BlockSpec((POST_TOKENS, D_PLE), tok),
            _const_spec((1, D_MODEL)),
            _const_spec((1, D_MODEL)),
            _const_spec((D_MODEL, D_MODEL)),
            _const_spec((1, D_MODEL)),
            _const_spec((D_PLE, D_MODEL)),
        ],
        out_specs=pl.BlockSpec((POST_TOKENS, D_MODEL), tok),
        out_shape=jax.ShapeDtypeStruct((t, D_MODEL), F32),
        compiler_params=_params("arbitrary"),
        name="post_layer",
    )(h, y, p, g2, b2, wg, bg, wp)


def _row(v):
    return v.reshape(1, -1).astype(F32)


def _sgu_spatial_params(w_s, b_s, dec_seq):
    tri = jnp.tril(jnp.ones((CHUNK, CHUNK), bool))
    ws_prompt = jnp.where(tri, w_s, 0.0)
    tri_s = jnp.tril(jnp.ones((dec_seq, dec_seq), bool))
    small = jnp.where(tri_s, w_s[:, :dec_seq, :dec_seq], 0.0)
    eye = jnp.eye(CHUNK // dec_seq, dtype=w_s.dtype)
    ws_sample = jnp.einsum("ab,gts->gatbs", eye, small).reshape(SGU_GROUPS, CHUNK, CHUNK)
    ws2 = jnp.stack([ws_prompt, ws_sample]).astype(BF16)
    bs_prompt = b_s.T
    bs_sample = jnp.tile(b_s[:, :dec_seq].T, (CHUNK // dec_seq, 1))
    bs2 = jnp.stack([bs_prompt, bs_sample])
    bs2 = jnp.repeat(bs2, SGU_GROUP_DIM, axis=2).astype(F32)
    return ws2, bs2


def kernel(x_prompt, x_sample, cache_k_win, cache_v_win, p_prompt, p_sample,
           ln1_g, ln1_b, ln2_g, ln2_b,
           sgu_w_in, sgu_b_in, sgu_ln_g, sgu_ln_b, sgu_w_s, sgu_b_s, sgu_w_out, sgu_b_out,
           attn_w_kv, attn_w_q, attn_sinks, attn_w_o,
           peer_w_q, peer_subkeys, peer_u, peer_v,
           ple_w, ple_gate_w, ple_gate_b):
    batch, seq, _ = x_prompt.shape
    dec_batch, dec_seq, _ = x_sample.shape
    tp = batch * seq
    ts = dec_batch * dec_seq
    half = N_KV_HEADS * HEAD_DIM
    wc = cache_k_win.shape[1]
    assert wc == WINDOW and CHUNK % dec_seq == 0 and seq % WINDOW == 0
    assert tp % PEER_TOKENS == 0 and ts % PEER_TOKENS == 0 and ts % SGU_TOKENS == 0

    h = jnp.concatenate([x_prompt.reshape(tp, D_MODEL), x_sample.reshape(ts, D_MODEL)], axis=0)
    p_all = jnp.concatenate([p_prompt.reshape(DEPTH, tp, D_PLE),
                             p_sample.reshape(DEPTH, ts, D_PLE)], axis=1)

    sgu_rows = []
    k_all = v_all = None
    for i in range(DEPTH):
        g1, b1 = _row(ln1_g[i]), _row(ln1_b[i])
        if i < N_A_LAYERS:
            ws2, bs2 = _sgu_spatial_params(sgu_w_s[i], sgu_b_s[i], dec_seq)
            h, v_rows = _sgu_layer(
                h, tp, sgu_w_in[i].astype(BF16), _row(sgu_b_in[i]), _row(sgu_ln_g[i]),
                _row(sgu_ln_b[i]), ws2, bs2, sgu_w_out[i].astype(BF16), _row(sgu_b_out[i]), g1, b1)
            sgu_rows.append(v_rows.reshape(dec_batch, dec_seq, D_SGU))
        else:
            if i == N_A_LAYERS:
                k_all, v_all = _kv_proj(h, attn_w_kv.astype(BF16))
                kp, vp = k_all[:tp], v_all[:tp]
                kn = k_all[tp:].reshape(dec_batch, dec_seq, half)
                vn = v_all[tp:].reshape(dec_batch, dec_seq, half)
                pad = ((0, 0), (0, WINDOW - dec_seq), (0, 0))
                kn_pad = jnp.pad(kn, pad).reshape(dec_batch * WINDOW, half)
                vn_pad = jnp.pad(vn, pad).reshape(dec_batch * WINDOW, half)
                ck = cache_k_win.reshape(dec_batch * wc, half).astype(F32)
                cv = cache_v_win.reshape(dec_batch * wc, half).astype(F32)
            jl = i - N_A_LAYERS
            wqT = attn_w_q[jl].T.astype(BF16)
            woT = attn_w_o[jl].T.astype(BF16)
            sinks = attn_sinks[jl].astype(F32)
            hp = _attn_layer(h[:tp], kp, kp, vp, vp, sinks, wqT, woT, g1, b1,
                             blocks_per_seq=seq // WINDOW)
            hs_pad = jnp.pad(h[tp:].reshape(dec_batch, dec_seq, D_MODEL),
                             ((0, 0), (0, WINDOW - dec_seq), (0, 0))).reshape(dec_batch * WINDOW, D_MODEL)
            hs = _attn_layer(hs_pad, ck, kn_pad, cv, vn_pad, sinks, wqT, woT, g1, b1,
                             blocks_per_seq=0)
            hs = hs.reshape(dec_batch, WINDOW, D_MODEL)[:, :dec_seq].reshape(ts, D_MODEL)
            h = jnp.concatenate([hp, hs], axis=0)
        y = _peer_layer(h, peer_w_q[i].T.astype(BF16), peer_subkeys[i].astype(BF16),
                        peer_u[i].astype(BF16), peer_v[i].T.astype(BF16))
        h = _post_layer(h, y, p_all[i], _row(ln2_g[i]), _row(ln2_b[i]),
                        ple_gate_w[i].astype(BF16), _row(ple_gate_b[i]), ple_w[i].astype(BF16))

    y_prompt = h[:tp].reshape(batch, seq, D_MODEL)
    y_sample = h[tp:].reshape(dec_batch, dec_seq, D_MODEL)
    kp4 = kp.reshape(batch, seq, N_KV_HEADS, HEAD_DIM)
    vp4 = vp.reshape(batch, seq, N_KV_HEADS, HEAD_DIM)
    new_k_win_prompt = kp4[:, -WINDOW:]
    new_v_win_prompt = vp4[:, -WINDOW:]
    kn4 = kn.reshape(dec_batch, dec_seq, N_KV_HEADS, HEAD_DIM)
    vn4 = vn.reshape(dec_batch, dec_seq, N_KV_HEADS, HEAD_DIM)
    new_k_win_sample = jnp.concatenate([cache_k_win.astype(F32), kn4], axis=1)[:, -wc:]
    new_v_win_sample = jnp.concatenate([cache_v_win.astype(F32), vn4], axis=1)[:, -wc:]
    new_sgu_v_sample = jnp.stack(sgu_rows, axis=0)
    return (y_prompt, y_sample, new_k_win_prompt, new_v_win_prompt,
            new_k_win_sample, new_v_win_sample, new_sgu_v_sample)
```

```python
import functools
import math

import jax
import jax.numpy as jnp
from jax import lax
from jax.experimental import pallas as pl
from jax.experimental.pallas import tpu as pltpu

D_MODEL = 1024
DEPTH = 4
N_A_LAYERS = DEPTH // 2
CHUNK = 128
D_SGU = 2 * D_MODEL
SGU_GROUPS = 8
SGU_GROUP_DIM = D_SGU // SGU_GROUPS
HEAD_DIM = 64
N_HEADS = D_MODEL // HEAD_DIM
N_KV_HEADS = 2
GROUP = N_HEADS // N_KV_HEADS
WINDOW = 128
PEER_HEADS = 8
N_KEYS = 128
N_EXPERTS = N_KEYS * N_KEYS
D_KEY = 256
PEER_TOPK = 16
D_PLE = 256
DEEPNORM_ALPHA = (2.0 * DEPTH) ** 0.25
LN_EPS = 1e-5

LANES = 128
MXU_DIM = 256
VMEM_LIMIT_BYTES = 56 * 1024 * 1024

SGU_TOKENS = 256
PEER_TOKENS = 512
PEER_EXPERTS = 1024
POST_TOKENS = 512
KV_TOKENS = 512

NEG_INF = float("-inf")
NT_DIMS = (((1,), (1,)), ((), ()))

BF16 = jnp.bfloat16
F32 = jnp.float32


def _layer_norm(x, g, b):
    mu = jnp.mean(x, axis=-1, keepdims=True)
    xc = x - mu
    var = jnp.mean(xc * xc, axis=-1, keepdims=True)
    return xc * lax.rsqrt(var + LN_EPS) * g + b


def _dot(a, b):
    return jnp.dot(a, b, preferred_element_type=F32)


def _dot_nt(a, b):
    return lax.dot_general(a, b, NT_DIMS, preferred_element_type=F32)


def _const_spec(shape):
    n = len(shape)
    return pl.BlockSpec(shape, lambda *_: (0,) * n)


def _params(*semantics, flags=None):
    return pltpu.CompilerParams(dimension_semantics=semantics,
                                vmem_limit_bytes=VMEM_LIMIT_BYTES, flags=flags)


def _sgu_kernel(x_ref, win_ref, bin_ref, lng_ref, lnb_ref, ws_ref, bs_ref,
                wout_ref, bout_ref, g1_ref, b1_ref, h_ref, v_ref,
                u_scr, v_scr):
    x = x_ref[...]
    xb = x.astype(BF16)
    n_col = (2 * D_SGU) // 512
    for j in range(n_col):
        z = _dot(xb, win_ref[:, j * 512:(j + 1) * 512]) + bin_ref[:, j * 512:(j + 1) * 512]
        z = jax.nn.gelu(z)
        if j < n_col // 2:
            u_scr[:, j * 512:(j + 1) * 512] = z
        else:
            jj = j - n_col // 2
            v_scr[:, jj * 512:(jj + 1) * 512] = z
    v = _layer_norm(v_scr[...], lng_ref[...], lnb_ref[...])
    v_ref[...] = v
    v_scr[...] = v
    for c in range(SGU_TOKENS // CHUNK):
        rows = slice(c * CHUNK, (c + 1) * CHUNK)
        for g in range(SGU_GROUPS):
            cols = slice(g * SGU_GROUP_DIM, (g + 1) * SGU_GROUP_DIM)
            mixed = _dot(ws_ref[0, g], v_scr[rows, cols].astype(BF16)) + bs_ref[0, :, cols]
            u_scr[rows, cols] = u_scr[rows, cols] * mixed
    out = _dot(u_scr[...].astype(BF16), wout_ref[...]) + bout_ref[...]
    h_ref[...] = _layer_norm(DEEPNORM_ALPHA * x + out, g1_ref[...], b1_ref[...])


def _sgu_layer(h, n_prompt_tokens, win, b_in, ln_g, ln_b, ws2, bs2, wout, b_out, g1, b1):
    t = h.shape[0]
    n_tiles = t // SGU_TOKENS
    n_prompt_tiles = n_prompt_tokens // SGU_TOKENS
    t_sample = t - n_prompt_tokens

    def kind(i):
        return jnp.where(i < n_prompt_tiles, 0, 1)

    return pl.pallas_call(
        _sgu_kernel,
        grid=(n_tiles,),
        in_specs=[
            pl.BlockSpec((SGU_TOKENS, D_MODEL), lambda i: (i, 0)),
            _const_spec((D_MODEL, 2 * D_SGU)),
            _const_spec((1, 2 * D_SGU)),
            _const_spec((1, D_SGU)),
            _const_spec((1, D_SGU)),
            pl.BlockSpec((1, SGU_GROUPS, CHUNK, CHUNK), lambda i: (kind(i), 0, 0, 0)),
            pl.BlockSpec((1, CHUNK, D_SGU), lambda i: (kind(i), 0, 0)),
            _const_spec((D_SGU, D_MODEL)),
            _const_spec((1, D_MODEL)),
            _const_spec((1, D_MODEL)),
            _const_spec((1, D_MODEL)),
        ],
        out_specs=[
            pl.BlockSpec((SGU_TOKENS, D_MODEL), lambda i: (i, 0)),
            pl.BlockSpec((SGU_TOKENS, D_SGU), lambda i: (jnp.maximum(i - n_prompt_tiles, 0), 0)),
        ],
        out_shape=[
            jax.ShapeDtypeStruct((t, D_MODEL), F32),
            jax.ShapeDtypeStruct((t_sample, D_SGU), F32),
        ],
        scratch_shapes=[
            pltpu.VMEM((SGU_TOKENS, D_SGU), F32),
            pltpu.VMEM((SGU_TOKENS, D_SGU), F32),
        ],
        compiler_params=_params("arbitrary"),
        name="sgu_layer",
    )(h, win, b_in, ln_g, ln_b, ws2, bs2, wout, b_out, g1, b1)


def _kv_kernel(h_ref, w_ref, k_ref, v_ref):
    kv = _dot(h_ref[...].astype(BF16), w_ref[...])
    half = N_KV_HEADS * HEAD_DIM
    k_ref[...] = kv[:, :half]
    v_ref[...] = kv[:, half:]


def _kv_proj(h, w_kv):
    t = h.shape[0]
    half = N_KV_HEADS * HEAD_DIM
    return pl.pallas_call(
        _kv_kernel,
        grid=(t // KV_TOKENS,),
        in_specs=[
            pl.BlockSpec((KV_TOKENS, D_MODEL), lambda i: (i, 0)),
            _const_spec((D_MODEL, 2 * half)),
        ],
        out_specs=[
            pl.BlockSpec((KV_TOKENS, half), lambda i: (i, 0)),
            pl.BlockSpec((KV_TOKENS, half), lambda i: (i, 0)),
        ],
        out_shape=[jax.ShapeDtypeStruct((t, half), F32)] * 2,
        compiler_params=_params("arbitrary"),
        name="kv_proj",
    )(h, w_kv)


def _attn_kernel(sink_ref, h_ref, kp_ref, kc_ref, vp_ref, vc_ref, wqT_ref, woT_ref,
                 g1_ref, b1_ref, o_ref, oT_scr, *, blocks_per_seq):
    hb = h_ref[...]
    qT = _dot_nt(wqT_ref[...], hb.astype(BF16)).astype(BF16)
    kk = jnp.concatenate([kp_ref[...], kc_ref[...]], axis=0).astype(BF16)
    vv = jnp.concatenate([vp_ref[...], vc_ref[...]], axis=0)
    vvT = vv.T.astype(BF16)

    kj = lax.broadcasted_iota(jnp.int32, (2 * WINDOW, WINDOW), 0)
    qi = lax.broadcasted_iota(jnp.int32, (2 * WINDOW, WINDOW), 1)
    dist = qi + WINDOW - kj
    valid = (dist >= 0) & (dist < WINDOW)
    if blocks_per_seq:
        is_first = (pl.program_id(0) % blocks_per_seq == 0).astype(jnp.int32)
        valid = valid & (kj >= is_first * WINDOW)
    distf = dist.astype(F32)

    zeros_q = jnp.zeros((HEAD_DIM, GROUP * WINDOW), BF16)
    for k in range(N_KV_HEADS):
        heads = range(k * GROUP, (k + 1) * GROUP)
        q_grp = jnp.concatenate([qT[h * HEAD_DIM:(h + 1) * HEAD_DIM] for h in heads], axis=1)
        q_pad = jnp.concatenate([q_grp, zeros_q] if k == 0 else [zeros_q, q_grp], axis=0)
        s_all = _dot(kk, q_pad) * (HEAD_DIM ** -0.5)
        p_blocks = []
        for g, h in enumerate(heads):
            slope = 2.0 ** (-8.0 * (h + 1) / N_HEADS)
            s = s_all[:, g * WINDOW:(g + 1) * WINDOW] - slope * distf
            s = jnp.where(valid, s, NEG_INF)
            sink = sink_ref[h]
            m = jnp.maximum(jnp.max(s, axis=0, keepdims=True), sink)
            p = jnp.exp(s - m)
            den = jnp.sum(p, axis=0, keepdims=True) + jnp.exp(sink - m)
            p_blocks.append((p / den).astype(BF16))
        o_all = _dot(vvT[k * HEAD_DIM:(k + 1) * HEAD_DIM],
                     jnp.concatenate(p_blocks, axis=1))
        for g, h in enumerate(heads):
            oT_scr[h * HEAD_DIM:(h + 1) * HEAD_DIM, :] = o_all[:, g * WINDOW:(g + 1) * WINDOW]
    outT = _dot(woT_ref[...], oT_scr[...].astype(BF16))
    o_ref[...] = _layer_norm(DEEPNORM_ALPHA * hb + outT.T, g1_ref[...], b1_ref[...])


def _attn_layer(h, k_prev_arr, k_cur_arr, v_prev_arr, v_cur_arr, sinks, wqT, woT, g1, b1,
                *, blocks_per_seq):
    t = h.shape[0]
    n_blocks = t // WINDOW
    half = N_KV_HEADS * HEAD_DIM
    if blocks_per_seq:
        prev_map = lambda i: (jnp.maximum(i - 1, 0), 0)
    else:
        prev_map = lambda i: (i, 0)
    cur_map = lambda i: (i, 0)
    return pl.pallas_call(
        functools.partial(_attn_kernel, blocks_per_seq=blocks_per_seq),
        grid=(n_blocks,),
        in_specs=[
            pl.BlockSpec(memory_space=pltpu.SMEM),
            pl.BlockSpec((WINDOW, D_MODEL), cur_map),
            pl.BlockSpec((WINDOW, half), prev_map),
            pl.BlockSpec((WINDOW, half), cur_map),
            pl.BlockSpec((WINDOW, half), prev_map),
            pl.BlockSpec((WINDOW, half), cur_map),
            _const_spec((D_MODEL, D_MODEL)),
            _const_spec((D_MODEL, D_MODEL)),
            _const_spec((1, D_MODEL)),
            _const_spec((1, D_MODEL)),
        ],
        out_specs=pl.BlockSpec((WINDOW, D_MODEL), cur_map),
        out_shape=jax.ShapeDtypeStruct((t, D_MODEL), F32),
        scratch_shapes=[pltpu.VMEM((D_MODEL, WINDOW), F32)],
        compiler_params=_params("arbitrary"),
        name="attn_layer",
    )(sinks, h, k_prev_arr, k_cur_arr, v_prev_arr, v_cur_arr, wqT, woT, g1, b1)


NOT_RANKED = 127.0


def _top16_rows(s):
    n = N_KEYS // 8
    v = [s[8 * i:8 * (i + 1)] for i in range(n)]
    k = 2
    while k <= n:
        j = k // 2
        while j >= 1:
            for i in range(n):
                l = i ^ j
                if l > i:
                    hi, lo = jnp.maximum(v[i], v[l]), jnp.minimum(v[i], v[l])
                    v[i], v[l] = (hi, lo) if (i & k) == 0 else (lo, hi)
            j //= 2
        k *= 2
    row_id = lax.broadcasted_iota(jnp.int32, (PEER_TOPK, LANES), 0)
    rows = []
    stacked = jnp.full((PEER_TOPK, LANES), NEG_INF, F32)
    for i in range(PEER_TOPK):
        m = jnp.max(v[0], axis=0, keepdims=True)
        rows.append(m)
        stacked = jnp.where(row_id == i, m, stacked)
        hit = v[0] == m
        for d in range(PEER_TOPK - 1 - i):
            v[d] = jnp.where(hit, v[d + 1], v[d])
    return rows, stacked


def _rank_of(s, rows):
    rank = jnp.full(s.shape, NOT_RANKED, F32)
    for i in reversed(range(len(rows))):
        rank = jnp.where(s >= rows[i], float(i), rank)
    return rank


def _pair_threshold(r1, t1, r2, t2):
    row8 = lax.broadcasted_iota(jnp.int32, (8, LANES), 0)
    cands = []
    for i in range(8):
        n = PEER_TOPK // (i + 1)
        c = r1[i] + t2[0:8]
        if n < 8:
            c = jnp.where(row8 < n, c, NEG_INF)
        cands.append(c)
    cands.append(r1[0] + t2[8:16])
    cands.append(t1[8:16] + r2[0])
    top = r1[0] + r2[0]
    z = jnp.zeros((1, LANES), F32)
    m = top
    for _ in range(PEER_TOPK):
        m = cands[0]
        for c in cands[1:]:
            m = jnp.maximum(m, c)
        m = jnp.max(m, axis=0, keepdims=True)
        z = z + jnp.exp(m - top)
        cands = [jnp.where(c == m, NEG_INF, c) for c in cands]
    return m, z


def _peer_select(x_ref, wqT_ref, keys_ref, xb_scr, qT_scr, s1_scr, s2_scr,
                 cnt_scr, e1_scr, rank2_scr, e2_scr):
    xb_scr[...] = x_ref[...].T.astype(BF16)
    qT_scr[...] = _dot(wqT_ref[...], xb_scr[...]).astype(BF16)
    for h in range(PEER_HEADS):
        for p, dst in ((0, s1_scr), (1, s2_scr)):
            r0 = (h * 2 + p) * (D_KEY // 2)
            dst[h] = _dot(keys_ref[p], qT_scr[r0:r0 + D_KEY // 2, :])

    def chunk_body(c, carry):
        lanes = pl.ds(pl.multiple_of(c * LANES, LANES), LANES)

        def one_head(h):
            s1 = s1_scr[h, :, lanes]
            s2 = s2_scr[h, :, lanes]
            r1, t1 = _top16_rows(s1)
            r2, t2 = _top16_rows(s2)
            tau, z = _pair_threshold(r1, t1, r2, t2)
            cnt_by_rank = jnp.zeros(t1.shape, F32)
            for jx in range(PEER_TOPK):
                cnt_by_rank = cnt_by_rank + jnp.where(t1 + r2[jx] >= tau, 1.0, 0.0)
            cnt = jnp.zeros(s1.shape, F32)
            for i in reversed(range(PEER_TOPK)):
                cnt = jnp.where(s1 >= r1[i], cnt_by_rank[i:i + 1], cnt)
            cnt_scr[h, :, lanes] = cnt
            e1_scr[h, :, lanes] = jnp.exp(s1 - r1[0]) * (1.0 / z)
            rank2_scr[h, :, lanes] = _rank_of(s2, r2).astype(BF16)
            e2_scr[h, :, lanes] = jnp.exp(s2 - r2[0]).astype(BF16)

        def head_pair_body(hp, carry2):
            one_head(2 * hp)
            one_head(2 * hp + 1)
            return carry2

        return lax.fori_loop(0, PEER_HEADS // 2, head_pair_body, carry)

    lax.fori_loop(0, PEER_TOKENS // LANES, chunk_body, 0)


def _peer_gate(j, cnt_scr, e1_scr, rank2_scr, e2_scr, w_scr):
    a_per_step = PEER_EXPERTS // N_KEYS
    a_rows = pl.ds(pl.multiple_of(j * a_per_step, a_per_step), a_per_step)
    zero = jnp.zeros((), BF16)
    for c in range(PEER_TOKENS // LANES):
        lanes = slice(c * LANES, (c + 1) * LANES)
        cnt_blk = [cnt_scr[h, a_rows, lanes] for h in range(PEER_HEADS)]
        e1_blk = [e1_scr[h, a_rows, lanes] for h in range(PEER_HEADS)]
        for al in range(a_per_step):
            w = None
            for h in range(PEER_HEADS):
                cnt = jnp.broadcast_to(cnt_blk[h][al:al + 1], (N_KEYS, LANES)).astype(BF16)
                e1 = jnp.broadcast_to(e1_blk[h][al:al + 1], (N_KEYS, LANES)).astype(BF16)
                wh = jnp.where(rank2_scr[h, :, lanes] < cnt, e2_scr[h, :, lanes] * e1, zero)
                w = wh if w is None else w + wh
            w_scr[al * N_KEYS:(al + 1) * N_KEYS, lanes] = w


def _peer_kernel(x_ref, wqT_ref, keys_ref, u_ref, vT_ref, o_ref,
                 xb_scr, qT_scr, s1_scr, s2_scr, cnt_scr, e1_scr, rank2_scr, e2_scr,
                 w_scr, hid_scr, acc_scr):
    j = pl.program_id(1)

    @pl.when(j == 0)
    def _start():
        _peer_select(x_ref, wqT_ref, keys_ref, xb_scr, qT_scr, s1_scr, s2_scr,
                     cnt_scr, e1_scr, rank2_scr, e2_scr)
        acc_scr[...] = jnp.zeros_like(acc_scr)

    hid_scr[...] = _dot(u_ref[...], xb_scr[...]).astype(BF16)
    _peer_gate(j, cnt_scr, e1_scr, rank2_scr, e2_scr, w_scr)
    total = None
    for k in range(PEER_EXPERTS // MXU_DIM):
        rows = slice(k * MXU_DIM, (k + 1) * MXU_DIM)
        g = jax.nn.gelu(hid_scr[rows, :]) * w_scr[rows, :]
        part = _dot(vT_ref[:, rows], g)
        total = part if total is None else total + part
    acc_scr[...] += total

    @pl.when(j == pl.num_programs(1) - 1)
    def _finish():
        o_ref[...] = acc_scr[...].T


def _peer_layer(h, wqT, keys, u_tab, vT_tab):
    t = h.shape[0]
    n_tiles = N_EXPERTS // PEER_EXPERTS
    return pl.pallas_call(
        _peer_kernel,
        grid=(t // PEER_TOKENS, n_tiles),
        in_specs=[
            pl.BlockSpec((PEER_TOKENS, D_MODEL), lambda i, j: (i, 0)),
            _const_spec((PEER_HEADS * D_KEY, D_MODEL)),
            _const_spec((2, N_KEYS, D_KEY // 2)),
            pl.BlockSpec((PEER_EXPERTS, D_MODEL), lambda i, j: (j, 0)),
            pl.BlockSpec((D_MODEL, PEER_EXPERTS), lambda i, j: (0, j)),
        ],
        out_specs=pl.BlockSpec((PEER_TOKENS, D_MODEL), lambda i, j: (i, 0)),
        out_shape=jax.ShapeDtypeStruct((t, D_MODEL), F32),
        scratch_shapes=[
            pltpu.VMEM((D_MODEL, PEER_TOKENS), BF16),
            pltpu.VMEM((PEER_HEADS * D_KEY, PEER_TOKENS), BF16),
            pltpu.VMEM((PEER_HEADS, N_KEYS, PEER_TOKENS), F32),
            pltpu.VMEM((PEER_HEADS, N_KEYS, PEER_TOKENS), F32),
            pltpu.VMEM((PEER_HEADS, N_KEYS, PEER_TOKENS), F32),
            pltpu.VMEM((PEER_HEADS, N_KEYS, PEER_TOKENS), F32),
            pltpu.VMEM((PEER_HEADS, N_KEYS, PEER_TOKENS), BF16),
            pltpu.VMEM((PEER_HEADS, N_KEYS, PEER_TOKENS), BF16),
            pltpu.VMEM((PEER_EXPERTS, PEER_TOKENS), BF16),
            pltpu.VMEM((PEER_EXPERTS, PEER_TOKENS), BF16),
            pltpu.VMEM((D_MODEL, PEER_TOKENS), F32),
        ],
        compiler_params=_params("arbitrary", "arbitrary"),
        name="peer_layer",
    )(h, wqT, keys, u_tab, vT_tab)


def _post_kernel(h_ref, y_ref, p_ref, g2_ref, b2_ref, wg_ref, bg_ref, wp_ref, o_ref):
    hn = _layer_norm(DEEPNORM_ALPHA * h_ref[...] + y_ref[...], g2_ref[...], b2_ref[...])
    gate = jax.nn.sigmoid(_dot(hn.astype(BF16), wg_ref[...]) + bg_ref[...])
    o_ref[...] = hn + gate * _dot(p_ref[...].astype(BF16), wp_ref[...])


def _post_layer(h, y, p, g2, b2, wg, bg, wp):
    t = h.shape[0]
    tok = lambda i: (i, 0)
    return pl.pallas_call(
        _post_kernel,
        grid=(t // POST_TOKENS,),
        in_specs=[
            pl.BlockSpec((POST_TOKENS, D_MODEL), tok),
            pl.BlockSpec((POST_TOKENS, D_MODEL), tok),
            pl.BlockSpec((POST_TOKENS, D_PLE), tok),
            _const_spec((1, D_MODEL)),
            _const_spec((1, D_MODEL)),
            _const_spec((D_MODEL, D_MODEL)),
            _const_spec((1, D_MODEL)),
            _const_spec((D_PLE, D_MODEL)),
        ],
        out_specs=pl.BlockSpec((POST_TOKENS, D_MODEL), tok),
        out_shape=jax.ShapeDtypeStruct((t, D_MODEL), F32),
        compiler_params=_params("arbitrary"),
        name="post_layer",
    )(h, y, p, g2, b2, wg, bg, wp)


def _row(v):
    return v.reshape(1, -1).astype(F32)


def _sgu_spatial_params(w_s, b_s, dec_seq):
    tri = jnp.tril(jnp.ones((CHUNK, CHUNK), bool))
    ws_prompt = jnp.where(tri, w_s, 0.0)
    tri_s = jnp.tril(jnp.ones((dec_seq, dec_seq), bool))
    small = jnp.where(tri_s, w_s[:, :dec_seq, :dec_seq], 0.0)
    eye = jnp.eye(CHUNK // dec_seq, dtype=w_s.dtype)
    ws_sample = jnp.einsum("ab,gts->gatbs", eye, small).reshape(SGU_GROUPS, CHUNK, CHUNK)
    ws2 = jnp.stack([ws_prompt, ws_sample]).astype(BF16)
    bs_prompt = b_s.T
    bs_sample = jnp.tile(b_s[:, :dec_seq].T, (CHUNK // dec_seq, 1))
    bs2 = jnp.stack([bs_prompt, bs_sample])
    bs2 = jnp.repeat(bs2, SGU_GROUP_DIM, axis=2).astype(F32)
    return ws2, bs2


def kernel(x_prompt, x_sample, cache_k_win, cache_v_win, p_prompt, p_sample,
           ln1_g, ln1_b, ln2_g, ln2_b,
           sgu_w_in, sgu_b_in, sgu_ln_g, sgu_ln_b, sgu_w_s, sgu_b_s, sgu_w_out, sgu_b_out,
           attn_w_kv, attn_w_q, attn_sinks, attn_w_o,
           peer_w_q, peer_subkeys, peer_u, peer_v,
           ple_w, ple_gate_w, ple_gate_b):
    batch, seq, _ = x_prompt.shape
    dec_batch, dec_seq, _ = x_sample.shape
    tp = batch * seq
    ts = dec_batch * dec_seq
    half = N_KV_HEADS * HEAD_DIM
    wc = cache_k_win.shape[1]
    assert wc == WINDOW and CHUNK % dec_seq == 0 and seq % WINDOW == 0
    assert tp % PEER_TOKENS == 0 and ts % PEER_TOKENS == 0 and ts % SGU_TOKENS == 0

    h = jnp.concatenate([x_prompt.reshape(tp, D_MODEL), x_sample.reshape(ts, D_MODEL)], axis=0)
    p_all = jnp.concatenate([p_prompt.reshape(DEPTH, tp, D_PLE),
                             p_sample.reshape(DEPTH, ts, D_PLE)], axis=1)

    sgu_rows = []
    k_all = v_all = None
    for i in range(DEPTH):
        g1, b1 = _row(ln1_g[i]), _row(ln1_b[i])
        if i < N_A_LAYERS:
            ws2, bs2 = _sgu_spatial_params(sgu_w_s[i], sgu_b_s[i], dec_seq)
            h, v_rows = _sgu_layer(
                h, tp, sgu_w_in[i].astype(BF16), _row(sgu_b_in[i]), _row(sgu_ln_g[i]),
                _row(sgu_ln_b[i]), ws2, bs2, sgu_w_out[i].astype(BF16), _row(sgu_b_out[i]), g1, b1)
            sgu_rows.append(v_rows.reshape(dec_batch, dec_seq, D_SGU))
        else:
            if i == N_A_LAYERS:
                k_all, v_all = _kv_proj(h, attn_w_kv.astype(BF16))
                kp, vp = k_all[:tp], v_all[:tp]
                kn = k_all[tp:].reshape(dec_batch, dec_seq, half)
                vn = v_all[tp:].reshape(dec_batch, dec_seq, half)
                pad = ((0, 0), (0, WINDOW - dec_seq), (0, 0))
                kn_pad = jnp.pad(kn, pad).reshape(dec_batch * WINDOW, half)
                vn_pad = jnp.pad(vn, pad).reshape(dec_batch * WINDOW, half)
                ck = cache_k_win.reshape(dec_batch * wc, half).astype(F32)
                cv = cache_v_win.reshape(dec_batch * wc, half).astype(F32)
            jl = i - N_A_LAYERS
            wqT = attn_w_q[jl].T.astype(BF16)
            woT = attn_w_o[jl].T.astype(BF16)
            sinks = attn_sinks[jl].astype(F32)
            hp = _attn_layer(h[:tp], kp, kp, vp, vp, sinks, wqT, woT, g1, b1,
                             blocks_per_seq=seq // WINDOW)
            hs_pad = jnp.pad(h[tp:].reshape(dec_batch, dec_seq, D_MODEL),
                             ((0, 0), (0, WINDOW - dec_seq), (0, 0))).reshape(dec_batch * WINDOW, D_MODEL)
            hs = _attn_layer(hs_pad, ck, kn_pad, cv, vn_pad, sinks, wqT, woT, g1, b1,
                             blocks_per_seq=0)
            hs = hs.reshape(dec_batch, WINDOW, D_MODEL)[:, :dec_seq].reshape(ts, D_MODEL)
            h = jnp.concatenate([hp, hs], axis=0)
        y = _peer_layer(h, peer_w_q[i].T.astype(BF16), peer_subkeys[i].astype(BF16),
                        peer_u[i].astype(BF16), peer_v[i].T.astype(BF16))
        h = _post_layer(h, y, p_all[i], _row(ln2_g[i]), _row(ln2_b[i]),
                        ple_gate_w[i].astype(BF16), _row(ple_gate_b[i]), ple_w[i].astype(BF16))

    y_prompt = h[:tp].reshape(batch, seq, D_MODEL)
    y_sample = h[tp:].reshape(dec_batch, dec_seq, D_MODEL)
    kp4 = kp.reshape(batch, seq, N_KV_HEADS, HEAD_DIM)
    vp4 = vp.reshape(batch, seq, N_KV_HEADS, HEAD_DIM)
    new_k_win_prompt = kp4[:, -WINDOW:]
    new_v_win_prompt = vp4[:, -WINDOW:]
    kn4 = kn.reshape(dec_batch, dec_seq, N_KV_HEADS, HEAD_DIM)
    vn4 = vn.reshape(dec_batch, dec_seq, N_KV_HEADS, HEAD_DIM)
    new_k_win_sample = jnp.concatenate([cache_k_win.astype(F32), kn4], axis=1)[:, -wc:]
    new_v_win_sample = jnp.concatenate([cache_v_win.astype(F32), vn4], axis=1)[:, -wc:]
    new_sgu_v_sample = jnp.stack(sgu_rows, axis=0)
    return (y_prompt, y_sample, new_k_win_prompt, new_v_win_prompt,
            new_k_win_sample, new_v_win_sample, new_sgu_v_sample)
```

```python
import functools
import math

import jax
import jax.numpy as jnp
from jax import lax
from jax.experimental import pallas as pl
from jax.experimental.pallas import tpu as pltpu

D_MODEL = 1024
DEPTH = 4
N_A_LAYERS = DEPTH // 2
CHUNK = 128
D_SGU = 2 * D_MODEL
SGU_GROUPS = 8
SGU_GROUP_DIM = D_SGU // SGU_GROUPS
HEAD_DIM = 64
N_HEADS = D_MODEL // HEAD_DIM
N_KV_HEADS = 2
GROUP = N_HEADS // N_KV_HEADS
WINDOW = 128
PEER_HEADS = 8
N_KEYS = 128
N_EXPERTS = N_KEYS * N_KEYS
D_KEY = 256
PEER_TOPK = 16
D_PLE = 256
DEEPNORM_ALPHA = (2.0 * DEPTH) ** 0.25
LN_EPS = 1e-5

LANES = 128
MXU_DIM = 256
VMEM_LIMIT_BYTES = 58 * 1024 * 1024

SGU_TOKENS = 256
PEER_TOKENS = 1024
PEER_EXPERTS = 1024
POST_TOKENS = 512
KV_TOKENS = 512

NEG_INF = float("-inf")
NT_DIMS = (((1,), (1,)), ((), ()))

BF16 = jnp.bfloat16
F32 = jnp.float32


def _layer_norm(x, g, b):
    mu = jnp.mean(x, axis=-1, keepdims=True)
    xc = x - mu
    var = jnp.mean(xc * xc, axis=-1, keepdims=True)
    return xc * lax.rsqrt(var + LN_EPS) * g + b


def _dot(a, b):
    return jnp.dot(a, b, preferred_element_type=F32)


def _dot_nt(a, b):
    return lax.dot_general(a, b, NT_DIMS, preferred_element_type=F32)


def _const_spec(shape):
    n = len(shape)
    return pl.BlockSpec(shape, lambda *_: (0,) * n)


def _params(*semantics, flags=None):
    return pltpu.CompilerParams(dimension_semantics=semantics,
                                vmem_limit_bytes=VMEM_LIMIT_BYTES, flags=flags)


def _sgu_kernel(x_ref, win_ref, bin_ref, lng_ref, lnb_ref, ws_ref, bs_ref,
                wout_ref, bout_ref, g1_ref, b1_ref, h_ref, v_ref,
                u_scr, v_scr):
    x = x_ref[...]
    xb = x.astype(BF16)
    n_col = (2 * D_SGU) // 512
    for j in range(n_col):
        z = _dot(xb, win_ref[:, j * 512:(j + 1) * 512]) + bin_ref[:, j * 512:(j + 1) * 512]
        z = jax.nn.gelu(z)
        if j < n_col // 2:
            u_scr[:, j * 512:(j + 1) * 512] = z
        else:
            jj = j - n_col // 2
            v_scr[:, jj * 512:(jj + 1) * 512] = z
    v = _layer_norm(v_scr[...], lng_ref[...], lnb_ref[...])
    v_ref[...] = v
    v_scr[...] = v
    for c in range(SGU_TOKENS // CHUNK):
        rows = slice(c * CHUNK, (c + 1) * CHUNK)
        for g in range(SGU_GROUPS):
            cols = slice(g * SGU_GROUP_DIM, (g + 1) * SGU_GROUP_DIM)
            mixed = _dot(ws_ref[0, g], v_scr[rows, cols].astype(BF16)) + bs_ref[0, :, cols]
            u_scr[rows, cols] = u_scr[rows, cols] * mixed
    out = _dot(u_scr[...].astype(BF16), wout_ref[...]) + bout_ref[...]
    h_ref[...] = _layer_norm(DEEPNORM_ALPHA * x + out, g1_ref[...], b1_ref[...])


def _sgu_layer(h, n_prompt_tokens, win, b_in, ln_g, ln_b, ws2, bs2, wout, b_out, g1, b1):
    t = h.shape[0]
    n_tiles = t // SGU_TOKENS
    n_prompt_tiles = n_prompt_tokens // SGU_TOKENS
    t_sample = t - n_prompt_tokens

    def kind(i):
        return jnp.where(i < n_prompt_tiles, 0, 1)

    return pl.pallas_call(
        _sgu_kernel,
        grid=(n_tiles,),
        in_specs=[
            pl.BlockSpec((SGU_TOKENS, D_MODEL), lambda i: (i, 0)),
            _const_spec((D_MODEL, 2 * D_SGU)),
            _const_spec((1, 2 * D_SGU)),
            _const_spec((1, D_SGU)),
            _const_spec((1, D_SGU)),
            pl.BlockSpec((1, SGU_GROUPS, CHUNK, CHUNK), lambda i: (kind(i), 0, 0, 0)),
            pl.BlockSpec((1, CHUNK, D_SGU), lambda i: (kind(i), 0, 0)),
            _const_spec((D_SGU, D_MODEL)),
            _const_spec((1, D_MODEL)),
            _const_spec((1, D_MODEL)),
            _const_spec((1, D_MODEL)),
        ],
        out_specs=[
            pl.BlockSpec((SGU_TOKENS, D_MODEL), lambda i: (i, 0)),
            pl.BlockSpec((SGU_TOKENS, D_SGU), lambda i: (jnp.maximum(i - n_prompt_tiles, 0), 0)),
        ],
        out_shape=[
            jax.ShapeDtypeStruct((t, D_MODEL), F32),
            jax.ShapeDtypeStruct((t_sample, D_SGU), F32),
        ],
        scratch_shapes=[
            pltpu.VMEM((SGU_TOKENS, D_SGU), F32),
            pltpu.VMEM((SGU_TOKENS, D_SGU), F32),
        ],
        compiler_params=_params("arbitrary"),
        name="sgu_layer",
    )(h, win, b_in, ln_g, ln_b, ws2, bs2, wout, b_out, g1, b1)


def _kv_kernel(h_ref, w_ref, k_ref, v_ref):
    kv = _dot(h_ref[...].astype(BF16), w_ref[...])
    half = N_KV_HEADS * HEAD_DIM
    k_ref[...] = kv[:, :half]
    v_ref[...] = kv[:, half:]


def _kv_proj(h, w_kv):
    t = h.shape[0]
    half = N_KV_HEADS * HEAD_DIM
    return pl.pallas_call(
        _kv_kernel,
        grid=(t // KV_TOKENS,),
        in_specs=[
            pl.BlockSpec((KV_TOKENS, D_MODEL), lambda i: (i, 0)),
            _const_spec((D_MODEL, 2 * half)),
        ],
        out_specs=[
            pl.BlockSpec((KV_TOKENS, half), lambda i: (i, 0)),
            pl.BlockSpec((KV_TOKENS, half), lambda i: (i, 0)),
        ],
        out_shape=[jax.ShapeDtypeStruct((t, half), F32)] * 2,
        compiler_params=_params("arbitrary"),
        name="kv_proj",
    )(h, w_kv)


def _attn_kernel(sink_ref, h_ref, kp_ref, kc_ref, vp_ref, vc_ref, wqT_ref, woT_ref,
                 g1_ref, b1_ref, o_ref, oT_scr, *, blocks_per_seq):
    hb = h_ref[...]
    qT = _dot_nt(wqT_ref[...], hb.astype(BF16)).astype(BF16)
    kk = jnp.concatenate([kp_ref[...], kc_ref[...]], axis=0).astype(BF16)
    vv = jnp.concatenate([vp_ref[...], vc_ref[...]], axis=0)
    vvT = vv.T.astype(BF16)

    kj = lax.broadcasted_iota(jnp.int32, (2 * WINDOW, WINDOW), 0)
    qi = lax.broadcasted_iota(jnp.int32, (2 * WINDOW, WINDOW), 1)
    dist = qi + WINDOW - kj
    valid = (dist >= 0) & (dist < WINDOW)
    if blocks_per_seq:
        is_first = (pl.program_id(0) % blocks_per_seq == 0).astype(jnp.int32)
        valid = valid & (kj >= is_first * WINDOW)
    distf = dist.astype(F32)

    zeros_q = jnp.zeros((HEAD_DIM, GROUP * WINDOW), BF16)
    for k in range(N_KV_HEADS):
        heads = range(k * GROUP, (k + 1) * GROUP)
        q_grp = jnp.concatenate([qT[h * HEAD_DIM:(h + 1) * HEAD_DIM] for h in heads], axis=1)
        q_pad = jnp.concatenate([q_grp, zeros_q] if k == 0 else [zeros_q, q_grp], axis=0)
        s_all = _dot(kk, q_pad) * (HEAD_DIM ** -0.5)
        p_blocks = []
        for g, h in enumerate(heads):
            slope = 2.0 ** (-8.0 * (h + 1) / N_HEADS)
            s = s_all[:, g * WINDOW:(g + 1) * WINDOW] - slope * distf
            s = jnp.where(valid, s, NEG_INF)
            sink = sink_ref[h]
            m = jnp.maximum(jnp.max(s, axis=0, keepdims=True), sink)
            p = jnp.exp(s - m)
            den = jnp.sum(p, axis=0, keepdims=True) + jnp.exp(sink - m)
            p_blocks.append((p / den).astype(BF16))
        o_all = _dot(vvT[k * HEAD_DIM:(k + 1) * HEAD_DIM],
                     jnp.concatenate(p_blocks, axis=1))
        for g, h in enumerate(heads):
            oT_scr[h * HEAD_DIM:(h + 1) * HEAD_DIM, :] = o_all[:, g * WINDOW:(g + 1) * WINDOW]
    outT = _dot(woT_ref[...], oT_scr[...].astype(BF16))
    o_ref[...] = _layer_norm(DEEPNORM_ALPHA * hb + outT.T, g1_ref[...], b1_ref[...])


def _attn_layer(h, k_prev_arr, k_cur_arr, v_prev_arr, v_cur_arr, sinks, wqT, woT, g1, b1,
                *, blocks_per_seq):
    t = h.shape[0]
    n_blocks = t // WINDOW
    half = N_KV_HEADS * HEAD_DIM
    if blocks_per_seq:
        prev_map = lambda i: (jnp.maximum(i - 1, 0), 0)
    else:
        prev_map = lambda i: (i, 0)
    cur_map = lambda i: (i, 0)
    return pl.pallas_call(
        functools.partial(_attn_kernel, blocks_per_seq=blocks_per_seq),
        grid=(n_blocks,),
        in_specs=[
            pl.BlockSpec(memory_space=pltpu.SMEM),
            pl.BlockSpec((WINDOW, D_MODEL), cur_map),
            pl.BlockSpec((WINDOW, half), prev_map),
            pl.BlockSpec((WINDOW, half), cur_map),
            pl.BlockSpec((WINDOW, half), prev_map),
            pl.BlockSpec((WINDOW, half), cur_map),
            _const_spec((D_MODEL, D_MODEL)),
            _const_spec((D_MODEL, D_MODEL)),
            _const_spec((1, D_MODEL)),
            _const_spec((1, D_MODEL)),
        ],
        out_specs=pl.BlockSpec((WINDOW, D_MODEL), cur_map),
        out_shape=jax.ShapeDtypeStruct((t, D_MODEL), F32),
        scratch_shapes=[pltpu.VMEM((D_MODEL, WINDOW), F32)],
        compiler_params=_params("arbitrary"),
        name="attn_layer",
    )(sinks, h, k_prev_arr, k_cur_arr, v_prev_arr, v_cur_arr, wqT, woT, g1, b1)


NOT_RANKED = 127.0


def _top16_rows(s):
    n = N_KEYS // 8
    v = [s[8 * i:8 * (i + 1)] for i in range(n)]
    k = 2
    while k <= n:
        j = k // 2
        while j >= 1:
            for i in range(n):
                l = i ^ j
                if l > i:
                    hi, lo = jnp.maximum(v[i], v[l]), jnp.minimum(v[i], v[l])
                    v[i], v[l] = (hi, lo) if (i & k) == 0 else (lo, hi)
            j //= 2
        k *= 2
    row_id = lax.broadcasted_iota(jnp.int32, (PEER_TOPK, LANES), 0)
    rows = []
    stacked = jnp.full((PEER_TOPK, LANES), NEG_INF, F32)
    for i in range(PEER_TOPK):
        m = jnp.max(v[0], axis=0, keepdims=True)
        rows.append(m)
        stacked = jnp.where(row_id == i, m, stacked)
        hit = v[0] == m
        for d in range(PEER_TOPK - 1 - i):
            v[d] = jnp.where(hit, v[d + 1], v[d])
    return rows, stacked


def _rank_of(s, rows):
    rank = jnp.full(s.shape, NOT_RANKED, F32)
    for i in reversed(range(len(rows))):
        rank = jnp.where(s >= rows[i], float(i), rank)
    return rank


def _pair_threshold(r1, t1, r2, t2):
    row8 = lax.broadcasted_iota(jnp.int32, (8, LANES), 0)
    cands = []
    for i in range(8):
        n = PEER_TOPK // (i + 1)
        c = r1[i] + t2[0:8]
        if n < 8:
            c = jnp.where(row8 < n, c, NEG_INF)
        cands.append(c)
    cands.append(r1[0] + t2[8:16])
    cands.append(t1[8:16] + r2[0])
    top = r1[0] + r2[0]
    z = jnp.zeros((1, LANES), F32)
    m = top
    for _ in range(PEER_TOPK):
        m = cands[0]
        for c in cands[1:]:
            m = jnp.maximum(m, c)
        m = jnp.max(m, axis=0, keepdims=True)
        z = z + jnp.exp(m - top)
        cands = [jnp.where(c == m, NEG_INF, c) for c in cands]
    return m, z


def _peer_select(x_ref, wqT_ref, keys_ref, xb_scr, qT_scr, cnt_scr, e1_scr, rank2_scr, e2_scr):
    xb_scr[...] = x_ref[...].T.astype(BF16)
    qT_scr[...] = _dot(wqT_ref[...], xb_scr[...]).astype(BF16)
    for h in range(PEER_HEADS):
        for p, dst in ((0, cnt_scr), (1, e1_scr)):
            r0 = (h * 2 + p) * (D_KEY // 2)
            dst[h] = _dot(keys_ref[p], qT_scr[r0:r0 + D_KEY // 2, :])

    def chunk_body(c, carry):
        lanes = pl.ds(pl.multiple_of(c * LANES, LANES), LANES)

        def one_head(h):
            s1 = cnt_scr[h, :, lanes]
            s2 = e1_scr[h, :, lanes]
            r1, t1 = _top16_rows(s1)
            r2, t2 = _top16_rows(s2)
            tau, z = _pair_threshold(r1, t1, r2, t2)
            cnt_by_rank = jnp.zeros(t1.shape, F32)
            for jx in range(PEER_TOPK):
                cnt_by_rank = cnt_by_rank + jnp.where(t1 + r2[jx] >= tau, 1.0, 0.0)
            cnt = jnp.zeros(s1.shape, F32)
            for i in reversed(range(PEER_TOPK)):
                cnt = jnp.where(s1 >= r1[i], cnt_by_rank[i:i + 1], cnt)
            cnt_scr[h, :, lanes] = cnt
            e1_scr[h, :, lanes] = jnp.exp(s1 - r1[0]) * (1.0 / z)
            rank2_scr[h, :, lanes] = _rank_of(s2, r2).astype(BF16)
            e2_scr[h, :, lanes] = jnp.exp(s2 - r2[0]).astype(BF16)

        def head_pair_body(hp, carry2):
            one_head(2 * hp)
            one_head(2 * hp + 1)
            return carry2

        return lax.fori_loop(0, PEER_HEADS // 2, head_pair_body, carry)

    lax.fori_loop(0, PEER_TOKENS // LANES, chunk_body, 0)


def _peer_gate(j, cnt_scr, e1_scr, rank2_scr, e2_scr, w_scr):
    a_per_step = PEER_EXPERTS // N_KEYS
    a_rows = pl.ds(pl.multiple_of(j * a_per_step, a_per_step), a_per_step)
    zero = jnp.zeros((), BF16)
    for c in range(PEER_TOKENS // LANES):
        lanes = slice(c * LANES, (c + 1) * LANES)
        cnt_blk = [cnt_scr[h, a_rows, lanes] for h in range(PEER_HEADS)]
        e1_blk = [e1_scr[h, a_rows, lanes] for h in range(PEER_HEADS)]
        for al in range(a_per_step):
            w = None
            for h in range(PEER_HEADS):
                cnt = jnp.broadcast_to(cnt_blk[h][al:al + 1], (N_KEYS, LANES)).astype(BF16)
                e1 = jnp.broadcast_to(e1_blk[h][al:al + 1], (N_KEYS, LANES)).astype(BF16)
                wh = jnp.where(rank2_scr[h, :, lanes] < cnt, e2_scr[h, :, lanes] * e1, zero)
                w = wh if w is None else w + wh
            w_scr[al * N_KEYS:(al + 1) * N_KEYS, lanes] = w


def _peer_kernel(x_ref, wqT_ref, keys_ref, u_ref, vT_ref, oT_ref,
                 xb_scr, qT_scr, cnt_scr, e1_scr, rank2_scr, e2_scr):
    j = pl.program_id(1)

    @pl.when(j == 0)
    def _start():
        _peer_select(x_ref, wqT_ref, keys_ref, xb_scr, qT_scr, cnt_scr, e1_scr, rank2_scr, e2_scr)
        oT_ref[...] = jnp.zeros_like(oT_ref)

    w_scr = qT_scr.at[0:PEER_EXPERTS]
    hid_scr = qT_scr.at[PEER_EXPERTS:2 * PEER_EXPERTS]
    hid_scr[...] = _dot(u_ref[...], xb_scr[...]).astype(BF16)
    _peer_gate(j, cnt_scr, e1_scr, rank2_scr, e2_scr, w_scr)
    total = None
    for k in range(PEER_EXPERTS // MXU_DIM):
        rows = slice(k * MXU_DIM, (k + 1) * MXU_DIM)
        g = jax.nn.gelu(hid_scr[rows, :]) * w_scr[rows, :]
        part = _dot(vT_ref[0, :, rows], g)
        total = part if total is None else total + part
    oT_ref[...] += total


def _peer_layer(h, wqT, keys, u_tab, vT_tab):
    t = h.shape[0]
    n_tiles = N_EXPERTS // PEER_EXPERTS
    assert PEER_HEADS * D_KEY == 2 * PEER_EXPERTS
    once = pl.Buffered(1)
    return pl.pallas_call(
        _peer_kernel,
        grid=(t // PEER_TOKENS, n_tiles),
        in_specs=[
            pl.BlockSpec((PEER_TOKENS, D_MODEL), lambda i, j: (i, 0), pipeline_mode=once),
            pl.BlockSpec((PEER_HEADS * D_KEY, D_MODEL), lambda i, j: (0, 0), pipeline_mode=once),
            pl.BlockSpec((2, N_KEYS, D_KEY // 2), lambda i, j: (0, 0, 0), pipeline_mode=once),
            pl.BlockSpec((PEER_EXPERTS, D_MODEL), lambda i, j: (j, 0)),
            pl.BlockSpec((1, D_MODEL, PEER_EXPERTS), lambda i, j: (j, 0, 0)),
        ],
        out_specs=pl.BlockSpec((D_MODEL, PEER_TOKENS), lambda i, j: (0, i)),
        out_shape=jax.ShapeDtypeStruct((D_MODEL, t), F32),
        scratch_shapes=[
            pltpu.VMEM((D_MODEL, PEER_TOKENS), BF16),
            pltpu.VMEM((PEER_HEADS * D_KEY, PEER_TOKENS), BF16),
            pltpu.VMEM((PEER_HEADS, N_KEYS, PEER_TOKENS), F32),
            pltpu.VMEM((PEER_HEADS, N_KEYS, PEER_TOKENS), F32),
            pltpu.VMEM((PEER_HEADS, N_KEYS, PEER_TOKENS), BF16),
            pltpu.VMEM((PEER_HEADS, N_KEYS, PEER_TOKENS), BF16),
        ],
        compiler_params=_params("arbitrary", "arbitrary"),
        name="peer_layer",
    )(h, wqT, keys, u_tab, vT_tab)


def _post_kernel(h_ref, yT_ref, p_ref, g2_ref, b2_ref, wg_ref, bg_ref, wp_ref, o_ref):
    hn = _layer_norm(DEEPNORM_ALPHA * h_ref[...] + yT_ref[...].T, g2_ref[...], b2_ref[...])
    gate = jax.nn.sigmoid(_dot(hn.astype(BF16), wg_ref[...]) + bg_ref[...])
    o_ref[...] = hn + gate * _dot(p_ref[...].astype(BF16), wp_ref[...])


def _post_layer(h, yT, p, g2, b2, wg, bg, wp):
    t = h.shape[0]
    tok = lambda i: (i, 0)
    return pl.pallas_call(
        _post_kernel,
        grid=(t // POST_TOKENS,),
        in_specs=[
            pl.BlockSpec((POST_TOKENS, D_MODEL), tok),
            pl.BlockSpec((D_MODEL, POST_TOKENS), lambda i: (0, i)),
            pl.BlockSpec((POST_TOKENS, D_PLE), tok),
            _const_spec((1, D_MODEL)),
            _const_spec((1, D_MODEL)),
            _const_spec((D_MODEL, D_MODEL)),
            _const_spec((1, D_MODEL)),
            _const_spec((D_PLE, D_MODEL)),
        ],
        out_specs=pl.BlockSpec((POST_TOKENS, D_MODEL), tok),
        out_shape=jax.ShapeDtypeStruct((t, D_MODEL), F32),
        compiler_params=_params("arbitrary"),
        name="post_layer",
    )(h, yT, p, g2, b2, wg, bg, wp)


def _row(v):
    return v.reshape(1, -1).astype(F32)


def _sgu_spatial_params(w_s, b_s, dec_seq):
    tri = jnp.tril(jnp.ones((CHUNK, CHUNK), bool))
    ws_prompt = jnp.where(tri, w_s, 0.0)
    tri_s = jnp.tril(jnp.ones((dec_seq, dec_seq), bool))
    small = jnp.where(tri_s, w_s[:, :dec_seq, :dec_seq], 0.0)
    eye = jnp.eye(CHUNK // dec_seq, dtype=w_s.dtype)
    ws_sample = jnp.einsum("ab,gts->gatbs", eye, small).reshape(SGU_GROUPS, CHUNK, CHUNK)
    ws2 = jnp.stack([ws_prompt, ws_sample]).astype(BF16)
    bs_prompt = b_s.T
    bs_sample = jnp.tile(b_s[:, :dec_seq].T, (CHUNK // dec_seq, 1))
    bs2 = jnp.stack([bs_prompt, bs_sample])
    bs2 = jnp.repeat(bs2, SGU_GROUP_DIM, axis=2).astype(F32)
    return ws2, bs2


def kernel(x_prompt, x_sample, cache_k_win, cache_v_win, p_prompt, p_sample,
           ln1_g, ln1_b, ln2_g, ln2_b,
           sgu_w_in, sgu_b_in, sgu_ln_g, sgu_ln_b, sgu_w_s, sgu_b_s, sgu_w_out, sgu_b_out,
           attn_w_kv, attn_w_q, attn_sinks, attn_w_o,
           peer_w_q, peer_subkeys, peer_u, peer_v,
           ple_w, ple_gate_w, ple_gate_b):
    batch, seq, _ = x_prompt.shape
    dec_batch, dec_seq, _ = x_sample.shape
    tp = batch * seq
    ts = dec_batch * dec_seq
    half = N_KV_HEADS * HEAD_DIM
    wc = cache_k_win.shape[1]
    assert wc == WINDOW and CHUNK % dec_seq == 0 and seq % WINDOW == 0
    assert tp % PEER_TOKENS == 0 and ts % PEER_TOKENS == 0 and ts % SGU_TOKENS == 0

    h = jnp.concatenate([x_prompt.reshape(tp, D_MODEL), x_sample.reshape(ts, D_MODEL)], axis=0)
    p_all = jnp.concatenate([p_prompt.reshape(DEPTH, tp, D_PLE),
                             p_sample.reshape(DEPTH, ts, D_PLE)], axis=1)

    sgu_rows = []
    k_all = v_all = None
    for i in range(DEPTH):
        g1, b1 = _row(ln1_g[i]), _row(ln1_b[i])
        if i < N_A_LAYERS:
            ws2, bs2 = _sgu_spatial_params(sgu_w_s[i], sgu_b_s[i], dec_seq)
            h, v_rows = _sgu_layer(
                h, tp, sgu_w_in[i].astype(BF16), _row(sgu_b_in[i]), _row(sgu_ln_g[i]),
                _row(sgu_ln_b[i]), ws2, bs2, sgu_w_out[i].astype(BF16), _row(sgu_b_out[i]), g1, b1)
            sgu_rows.append(v_rows.reshape(dec_batch, dec_seq, D_SGU))
        else:
            if i == N_A_LAYERS:
                k_all, v_all = _kv_proj(h, attn_w_kv.astype(BF16))
                kp, vp = k_all[:tp], v_all[:tp]
                kn = k_all[tp:].reshape(dec_batch, dec_seq, half)
                vn = v_all[tp:].reshape(dec_batch, dec_seq, half)
                pad = ((0, 0), (0, WINDOW - dec_seq), (0, 0))
                kn_pad = jnp.pad(kn, pad).reshape(dec_batch * WINDOW, half)
                vn_pad = jnp.pad(vn, pad).reshape(dec_batch * WINDOW, half)
                ck = cache_k_win.reshape(dec_batch * wc, half).astype(F32)
                cv = cache_v_win.reshape(dec_batch * wc, half).astype(F32)
            jl = i - N_A_LAYERS
            wqT = attn_w_q[jl].T.astype(BF16)
            woT = attn_w_o[jl].T.astype(BF16)
            sinks = attn_sinks[jl].astype(F32)
            hp = _attn_layer(h[:tp], kp, kp, vp, vp, sinks, wqT, woT, g1, b1,
                             blocks_per_seq=seq // WINDOW)
            hs_pad = jnp.pad(h[tp:].reshape(dec_batch, dec_seq, D_MODEL),
                             ((0, 0), (0, WINDOW - dec_seq), (0, 0))).reshape(dec_batch * WINDOW, D_MODEL)
            hs = _attn_layer(hs_pad, ck, kn_pad, cv, vn_pad, sinks, wqT, woT, g1, b1,
                             blocks_per_seq=0)
            hs = hs.reshape(dec_batch, WINDOW, D_MODEL)[:, :dec_seq].reshape(ts, D_MODEL)
            h = jnp.concatenate([hp, hs], axis=0)
        y = _peer_layer(h, peer_w_q[i].T.astype(BF16), peer_subkeys[i].astype(BF16),
                        peer_u[i].astype(BF16),
                        peer_v[i].reshape(N_EXPERTS // PEER_EXPERTS, PEER_EXPERTS, D_MODEL)
                        .transpose(0, 2, 1).astype(BF16))
        h = _post_layer(h, y, p_all[i], _row(ln2_g[i]), _row(ln2_b[i]),
                        ple_gate_w[i].astype(BF16), _row(ple_gate_b[i]), ple_w[i].astype(BF16))

    y_prompt = h[:tp].reshape(batch, seq, D_MODEL)
    y_sample = h[tp:].reshape(dec_batch, dec_seq, D_MODEL)
    kp4 = kp.reshape(batch, seq, N_KV_HEADS, HEAD_DIM)
    vp4 = vp.reshape(batch, seq, N_KV_HEADS, HEAD_DIM)
    new_k_win_prompt = kp4[:, -WINDOW:]
    new_v_win_prompt = vp4[:, -WINDOW:]
    kn4 = kn.reshape(dec_batch, dec_seq, N_KV_HEADS, HEAD_DIM)
    vn4 = vn.reshape(dec_batch, dec_seq, N_KV_HEADS, HEAD_DIM)
    new_k_win_sample = jnp.concatenate([cache_k_win.astype(F32), kn4], axis=1)[:, -wc:]
    new_v_win_sample = jnp.concatenate([cache_v_win.astype(F32), vn4], axis=1)[:, -wc:]
    new_sgu_v_sample = jnp.stack(sgu_rows, axis=0)
    return (y_prompt, y_sample, new_k_win_prompt, new_v_win_prompt,
            new_k_win_sample, new_v_win_sample, new_sgu_v_sample)
```

```python
import functools
import math

import jax
import jax.numpy as jnp
from jax import lax
from jax.experimental import pallas as pl
from jax.experimental.pallas import tpu as pltpu

D_MODEL = 1024
DEPTH = 4
N_A_LAYERS = DEPTH // 2
CHUNK = 128
D_SGU = 2 * D_MODEL
SGU_GROUPS = 8
SGU_GROUP_DIM = D_SGU // SGU_GROUPS
HEAD_DIM = 64
N_HEADS = D_MODEL // HEAD_DIM
N_KV_HEADS = 2
GROUP = N_HEADS // N_KV_HEADS
WINDOW = 128
PEER_HEADS = 8
N_KEYS = 128
N_EXPERTS = N_KEYS * N_KEYS
D_KEY = 256
PEER_TOPK = 16
D_PLE = 256
DEEPNORM_ALPHA = (2.0 * DEPTH) ** 0.25
LN_EPS = 1e-5

LANES = 128
MXU_DIM = 256
VMEM_LIMIT_BYTES = 58 * 1024 * 1024

SGU_TOKENS = 256
PEER_TOKENS = 512
PEER_EXPERTS = 1024
POST_TOKENS = 512
KV_TOKENS = 512

NEG_INF = float("-inf")
NT_DIMS = (((1,), (1,)), ((), ()))

BF16 = jnp.bfloat16
F32 = jnp.float32


def _layer_norm(x, g, b):
    mu = jnp.mean(x, axis=-1, keepdims=True)
    xc = x - mu
    var = jnp.mean(xc * xc, axis=-1, keepdims=True)
    return xc * lax.rsqrt(var + LN_EPS) * g + b


def _dot(a, b):
    return jnp.dot(a, b, preferred_element_type=F32)


def _dot_nt(a, b):
    return lax.dot_general(a, b, NT_DIMS, preferred_element_type=F32)


def _const_spec(shape):
    n = len(shape)
    return pl.BlockSpec(shape, lambda *_: (0,) * n)


def _params(*semantics, flags=None):
    return pltpu.CompilerParams(dimension_semantics=semantics,
                                vmem_limit_bytes=VMEM_LIMIT_BYTES, flags=flags)


def _sgu_kernel(x_ref, win_ref, bin_ref, lng_ref, lnb_ref, ws_ref, bs_ref,
                wout_ref, bout_ref, g1_ref, b1_ref, h_ref, v_ref,
                u_scr, v_scr):
    x = x_ref[...]
    xb = x.astype(BF16)
    n_col = (2 * D_SGU) // 512
    for j in range(n_col):
        z = _dot(xb, win_ref[:, j * 512:(j + 1) * 512]) + bin_ref[:, j * 512:(j + 1) * 512]
        z = jax.nn.gelu(z)
        if j < n_col // 2:
            u_scr[:, j * 512:(j + 1) * 512] = z
        else:
            jj = j - n_col // 2
            v_scr[:, jj * 512:(jj + 1) * 512] = z
    v = _layer_norm(v_scr[...], lng_ref[...], lnb_ref[...])
    v_ref[...] = v
    v_scr[...] = v
    for c in range(SGU_TOKENS // CHUNK):
        rows = slice(c * CHUNK, (c + 1) * CHUNK)
        for g in range(SGU_GROUPS):
            cols = slice(g * SGU_GROUP_DIM, (g + 1) * SGU_GROUP_DIM)
            mixed = _dot(ws_ref[0, g], v_scr[rows, cols].astype(BF16)) + bs_ref[0, :, cols]
            u_scr[rows, cols] = u_scr[rows, cols] * mixed
    out = _dot(u_scr[...].astype(BF16), wout_ref[...]) + bout_ref[...]
    h_ref[...] = _layer_norm(DEEPNORM_ALPHA * x + out, g1_ref[...], b1_ref[...])


def _sgu_layer(h, n_prompt_tokens, win, b_in, ln_g, ln_b, ws2, bs2, wout, b_out, g1, b1):
    t = h.shape[0]
    n_tiles = t // SGU_TOKENS
    n_prompt_tiles = n_prompt_tokens // SGU_TOKENS
    t_sample = t - n_prompt_tokens

    def kind(i):
        return jnp.where(i < n_prompt_tiles, 0, 1)

    return pl.pallas_call(
        _sgu_kernel,
        grid=(n_tiles,),
        in_specs=[
            pl.BlockSpec((SGU_TOKENS, D_MODEL), lambda i: (i, 0)),
            _const_spec((D_MODEL, 2 * D_SGU)),
            _const_spec((1, 2 * D_SGU)),
            _const_spec((1, D_SGU)),
            _const_spec((1, D_SGU)),
            pl.BlockSpec((1, SGU_GROUPS, CHUNK, CHUNK), lambda i: (kind(i), 0, 0, 0)),
            pl.BlockSpec((1, CHUNK, D_SGU), lambda i: (kind(i), 0, 0)),
            _const_spec((D_SGU, D_MODEL)),
            _const_spec((1, D_MODEL)),
            _const_spec((1, D_MODEL)),
            _const_spec((1, D_MODEL)),
        ],
        out_specs=[
            pl.BlockSpec((SGU_TOKENS, D_MODEL), lambda i: (i, 0)),
            pl.BlockSpec((SGU_TOKENS, D_SGU), lambda i: (jnp.maximum(i - n_prompt_tiles, 0), 0)),
        ],
        out_shape=[
            jax.ShapeDtypeStruct((t, D_MODEL), F32),
            jax.ShapeDtypeStruct((t_sample, D_SGU), F32),
        ],
        scratch_shapes=[
            pltpu.VMEM((SGU_TOKENS, D_SGU), F32),
            pltpu.VMEM((SGU_TOKENS, D_SGU), F32),
        ],
        compiler_params=_params("arbitrary"),
        name="sgu_layer",
    )(h, win, b_in, ln_g, ln_b, ws2, bs2, wout, b_out, g1, b1)


def _kv_kernel(h_ref, w_ref, k_ref, v_ref):
    kv = _dot(h_ref[...].astype(BF16), w_ref[...])
    half = N_KV_HEADS * HEAD_DIM
    k_ref[...] = kv[:, :half]
    v_ref[...] = kv[:, half:]


def _kv_proj(h, w_kv):
    t = h.shape[0]
    half = N_KV_HEADS * HEAD_DIM
    return pl.pallas_call(
        _kv_kernel,
        grid=(t // KV_TOKENS,),
        in_specs=[
            pl.BlockSpec((KV_TOKENS, D_MODEL), lambda i: (i, 0)),
            _const_spec((D_MODEL, 2 * half)),
        ],
        out_specs=[
            pl.BlockSpec((KV_TOKENS, half), lambda i: (i, 0)),
            pl.BlockSpec((KV_TOKENS, half), lambda i: (i, 0)),
        ],
        out_shape=[jax.ShapeDtypeStruct((t, half), F32)] * 2,
        compiler_params=_params("arbitrary"),
        name="kv_proj",
    )(h, w_kv)


def _attn_kernel(sink_ref, h_ref, kp_ref, kc_ref, vp_ref, vc_ref, wqT_ref, woT_ref,
                 g1_ref, b1_ref, o_ref, oT_scr, *, blocks_per_seq):
    hb = h_ref[...]
    qT = _dot_nt(wqT_ref[...], hb.astype(BF16)).astype(BF16)
    kk = jnp.concatenate([kp_ref[...], kc_ref[...]], axis=0).astype(BF16)
    vv = jnp.concatenate([vp_ref[...], vc_ref[...]], axis=0)
    vvT = vv.T.astype(BF16)

    kj = lax.broadcasted_iota(jnp.int32, (2 * WINDOW, WINDOW), 0)
    qi = lax.broadcasted_iota(jnp.int32, (2 * WINDOW, WINDOW), 1)
    dist = qi + WINDOW - kj
    valid = (dist >= 0) & (dist < WINDOW)
    if blocks_per_seq:
        is_first = (pl.program_id(0) % blocks_per_seq == 0).astype(jnp.int32)
        valid = valid & (kj >= is_first * WINDOW)
    distf = dist.astype(F32)

    zeros_q = jnp.zeros((HEAD_DIM, GROUP * WINDOW), BF16)
    for k in range(N_KV_HEADS):
        heads = range(k * GROUP, (k + 1) * GROUP)
        q_grp = jnp.concatenate([qT[h * HEAD_DIM:(h + 1) * HEAD_DIM] for h in heads], axis=1)
        q_pad = jnp.concatenate([q_grp, zeros_q] if k == 0 else [zeros_q, q_grp], axis=0)
        s_all = _dot(kk, q_pad) * (HEAD_DIM ** -0.5)
        p_blocks = []
        for g, h in enumerate(heads):
            slope = 2.0 ** (-8.0 * (h + 1) / N_HEADS)
            s = s_all[:, g * WINDOW:(g + 1) * WINDOW] - slope * distf
            s = jnp.where(valid, s, NEG_INF)
            sink = sink_ref[h]
            m = jnp.maximum(jnp.max(s, axis=0, keepdims=True), sink)
            p = jnp.exp(s - m)
            den = jnp.sum(p, axis=0, keepdims=True) + jnp.exp(sink - m)
            p_blocks.append((p / den).astype(BF16))
        o_all = _dot(vvT[k * HEAD_DIM:(k + 1) * HEAD_DIM],
                     jnp.concatenate(p_blocks, axis=1))
        for g, h in enumerate(heads):
            oT_scr[h * HEAD_DIM:(h + 1) * HEAD_DIM, :] = o_all[:, g * WINDOW:(g + 1) * WINDOW]
    outT = _dot(woT_ref[...], oT_scr[...].astype(BF16))
    o_ref[...] = _layer_norm(DEEPNORM_ALPHA * hb + outT.T, g1_ref[...], b1_ref[...])


def _attn_layer(h, k_prev_arr, k_cur_arr, v_prev_arr, v_cur_arr, sinks, wqT, woT, g1, b1,
                *, blocks_per_seq):
    t = h.shape[0]
    n_blocks = t // WINDOW
    half = N_KV_HEADS * HEAD_DIM
    if blocks_per_seq:
        prev_map = lambda i: (jnp.maximum(i - 1, 0), 0)
    else:
        prev_map = lambda i: (i, 0)
    cur_map = lambda i: (i, 0)
    return pl.pallas_call(
        functools.partial(_attn_kernel, blocks_per_seq=blocks_per_seq),
        grid=(n_blocks,),
        in_specs=[
            pl.BlockSpec(memory_space=pltpu.SMEM),
            pl.BlockSpec((WINDOW, D_MODEL), cur_map),
            pl.BlockSpec((WINDOW, half), prev_map),
            pl.BlockSpec((WINDOW, half), cur_map),
            pl.BlockSpec((WINDOW, half), prev_map),
            pl.BlockSpec((WINDOW, half), cur_map),
            _const_spec((D_MODEL, D_MODEL)),
            _const_spec((D_MODEL, D_MODEL)),
            _const_spec((1, D_MODEL)),
            _const_spec((1, D_MODEL)),
        ],
        out_specs=pl.BlockSpec((WINDOW, D_MODEL), cur_map),
        out_shape=jax.ShapeDtypeStruct((t, D_MODEL), F32),
        scratch_shapes=[pltpu.VMEM((D_MODEL, WINDOW), F32)],
        compiler_params=_params("arbitrary"),
        name="attn_layer",
    )(sinks, h, k_prev_arr, k_cur_arr, v_prev_arr, v_cur_arr, wqT, woT, g1, b1)


NOT_RANKED = 127.0


def _top16_rows(s):
    n = N_KEYS // 8
    v = [s[8 * i:8 * (i + 1)] for i in range(n)]
    k = 2
    while k <= n:
        j = k // 2
        while j >= 1:
            for i in range(n):
                l = i ^ j
                if l > i:
                    hi, lo = jnp.maximum(v[i], v[l]), jnp.minimum(v[i], v[l])
                    v[i], v[l] = (hi, lo) if (i & k) == 0 else (lo, hi)
            j //= 2
        k *= 2
    row_id = lax.broadcasted_iota(jnp.int32, (PEER_TOPK, LANES), 0)
    rows = []
    stacked = jnp.full((PEER_TOPK, LANES), NEG_INF, F32)
    for i in range(PEER_TOPK):
        m = jnp.max(v[0], axis=0, keepdims=True)
        rows.append(m)
        stacked = jnp.where(row_id == i, m, stacked)
        hit = v[0] == m
        for d in range(PEER_TOPK - 1 - i):
            v[d] = jnp.where(hit, v[d + 1], v[d])
    return rows, stacked


def _rank_of(s, rows):
    rank = jnp.full(s.shape, NOT_RANKED, F32)
    for i in reversed(range(len(rows))):
        rank = jnp.where(s >= rows[i], float(i), rank)
    return rank


def _pair_threshold(r1, t1, r2, t2):
    row8 = lax.broadcasted_iota(jnp.int32, (8, LANES), 0)
    cands = []
    for i in range(8):
        n = PEER_TOPK // (i + 1)
        c = r1[i] + t2[0:8]
        if n < 8:
            c = jnp.where(row8 < n, c, NEG_INF)
        cands.append(c)
    cands.append(r1[0] + t2[8:16])
    cands.append(t1[8:16] + r2[0])
    top = r1[0] + r2[0]
    z = jnp.zeros((1, LANES), F32)
    m = top
    for _ in range(PEER_TOPK):
        m = cands[0]
        for c in cands[1:]:
            m = jnp.maximum(m, c)
        m = jnp.max(m, axis=0, keepdims=True)
        z = z + jnp.exp(m - top)
        cands = [jnp.where(c == m, NEG_INF, c) for c in cands]
    return m, z


def _peer_select(x_ref, wqT_ref, keys_ref, xb_scr, qT_scr, cnt_scr, e1_scr, rank2_scr, e2_scr):
    xb_scr[...] = x_ref[...].T.astype(BF16)
    qT_scr[...] = _dot(wqT_ref[...], xb_scr[...]).astype(BF16)
    for h in range(PEER_HEADS):
        for p, dst in ((0, cnt_scr), (1, e1_scr)):
            r0 = (h * 2 + p) * (D_KEY // 2)
            s = _dot(keys_ref[p], qT_scr[r0:r0 + D_KEY // 2, :])
            for c in range(PEER_TOKENS // LANES):
                dst[h, c] = s[:, c * LANES:(c + 1) * LANES]

    def chunk_body(c, carry):
        def one_head(h):
            s1 = cnt_scr[h, c]
            s2 = e1_scr[h, c]
            r1, t1 = _top16_rows(s1)
            r2, t2 = _top16_rows(s2)
            tau, z = _pair_threshold(r1, t1, r2, t2)
            cnt_by_rank = jnp.zeros(t1.shape, F32)
            for jx in range(PEER_TOPK):
                cnt_by_rank = cnt_by_rank + jnp.where(t1 + r2[jx] >= tau, 1.0, 0.0)
            cnt = jnp.zeros(s1.shape, F32)
            for i in reversed(range(PEER_TOPK)):
                cnt = jnp.where(s1 >= r1[i], cnt_by_rank[i:i + 1], cnt)
            cnt_scr[h, c] = cnt
            e1_scr[h, c] = jnp.exp(s1 - r1[0]) * (1.0 / z)
            rank2_scr[h, c] = _rank_of(s2, r2).astype(BF16)
            e2_scr[h, c] = jnp.exp(s2 - r2[0]).astype(BF16)

        def head_pair_body(hp, carry2):
            one_head(2 * hp)
            one_head(2 * hp + 1)
            return carry2

        return lax.fori_loop(0, PEER_HEADS // 2, head_pair_body, carry)

    lax.fori_loop(0, PEER_TOKENS // LANES, chunk_body, 0)


def _peer_gate(j, cnt_scr, e1_scr, rank2_scr, e2_scr, w_scr):
    a_per_step = PEER_EXPERTS // N_KEYS
    a_rows = pl.ds(pl.multiple_of(j * a_per_step, a_per_step), a_per_step)
    zero = jnp.zeros((), BF16)
    for c in range(PEER_TOKENS // LANES):
        cnt_blk = [cnt_scr[h, c, a_rows, :] for h in range(PEER_HEADS)]
        e1_blk = [e1_scr[h, c, a_rows, :] for h in range(PEER_HEADS)]
        for al in range(a_per_step):
            w = None
            for h in range(PEER_HEADS):
                cnt = jnp.broadcast_to(cnt_blk[h][al:al + 1], (N_KEYS, LANES)).astype(BF16)
                e1 = jnp.broadcast_to(e1_blk[h][al:al + 1], (N_KEYS, LANES)).astype(BF16)
                wh = jnp.where(rank2_scr[h, c] < cnt, e2_scr[h, c] * e1, zero)
                w = wh if w is None else w + wh
            w_scr[c, al * N_KEYS:(al + 1) * N_KEYS, :] = w


def _peer_kernel(x_ref, wqT_ref, keys_ref, u_ref, vT_ref, oT_ref,
                 xb_scr, qT_scr, cnt_scr, e1_scr, rank2_scr, e2_scr, w_scr):
    j = pl.program_id(1)

    @pl.when(j == 0)
    def _start():
        _peer_select(x_ref, wqT_ref, keys_ref, xb_scr, qT_scr, cnt_scr, e1_scr, rank2_scr, e2_scr)
        oT_ref[...] = jnp.zeros_like(oT_ref)

    hid_scr = qT_scr.at[0:PEER_EXPERTS]
    hid_scr[...] = _dot(u_ref[...], xb_scr[...]).astype(BF16)
    _peer_gate(j, cnt_scr, e1_scr, rank2_scr, e2_scr, w_scr)
    total = None
    for k in range(PEER_EXPERTS // MXU_DIM):
        rows = slice(k * MXU_DIM, (k + 1) * MXU_DIM)
        w = jnp.concatenate([w_scr[c, rows, :] for c in range(PEER_TOKENS // LANES)], axis=1)
        g = jax.nn.gelu(hid_scr[rows, :]) * w
        part = _dot(vT_ref[0, :, rows], g)
        total = part if total is None else total + part
    oT_ref[...] += total


def _peer_layer(h, wqT, keys, u_tab, vT_tab):
    t = h.shape[0]
    n_tiles = N_EXPERTS // PEER_EXPERTS
    assert PEER_HEADS * D_KEY >= PEER_EXPERTS
    n_chunks = PEER_TOKENS // LANES
    once = pl.Buffered(1)
    return pl.pallas_call(
        _peer_kernel,
        grid=(t // PEER_TOKENS, n_tiles),
        in_specs=[
            pl.BlockSpec((PEER_TOKENS, D_MODEL), lambda i, j: (i, 0)),
            pl.BlockSpec((PEER_HEADS * D_KEY, D_MODEL), lambda i, j: (0, 0), pipeline_mode=once),
            pl.BlockSpec((2, N_KEYS, D_KEY // 2), lambda i, j: (0, 0, 0), pipeline_mode=once),
            pl.BlockSpec((PEER_EXPERTS, D_MODEL), lambda i, j: (j, 0)),
            pl.BlockSpec((1, D_MODEL, PEER_EXPERTS), lambda i, j: (j, 0, 0)),
        ],
        out_specs=pl.BlockSpec((D_MODEL, PEER_TOKENS), lambda i, j: (0, i)),
        out_shape=jax.ShapeDtypeStruct((D_MODEL, t), F32),
        scratch_shapes=[
            pltpu.VMEM((D_MODEL, PEER_TOKENS), BF16),
            pltpu.VMEM((PEER_HEADS * D_KEY, PEER_TOKENS), BF16),
            pltpu.VMEM((PEER_HEADS, n_chunks, N_KEYS, LANES), F32),
            pltpu.VMEM((PEER_HEADS, n_chunks, N_KEYS, LANES), F32),
            pltpu.VMEM((PEER_HEADS, n_chunks, N_KEYS, LANES), BF16),
            pltpu.VMEM((PEER_HEADS, n_chunks, N_KEYS, LANES), BF16),
            pltpu.VMEM((n_chunks, PEER_EXPERTS, LANES), BF16),
        ],
        compiler_params=_params("arbitrary", "arbitrary"),
        name="peer_layer",
    )(h, wqT, keys, u_tab, vT_tab)


def _post_kernel(h_ref, yT_ref, p_ref, g2_ref, b2_ref, wg_ref, bg_ref, wp_ref, o_ref):
    hn = _layer_norm(DEEPNORM_ALPHA * h_ref[...] + yT_ref[...].T, g2_ref[...], b2_ref[...])
    gate = jax.nn.sigmoid(_dot(hn.astype(BF16), wg_ref[...]) + bg_ref[...])
    o_ref[...] = hn + gate * _dot(p_ref[...].astype(BF16), wp_ref[...])


def _post_layer(h, yT, p, g2, b2, wg, bg, wp):
    t = h.shape[0]
    tok = lambda i: (i, 0)
    return pl.pallas_call(
        _post_kernel,
        grid=(t // POST_TOKENS,),
        in_specs=[
            pl.BlockSpec((POST_TOKENS, D_MODEL), tok),
            pl.BlockSpec((D_MODEL, POST_TOKENS), lambda i: (0, i)),
            pl.BlockSpec((POST_TOKENS, D_PLE), tok),
            _const_spec((1, D_MODEL)),
            _const_spec((1, D_MODEL)),
            _const_spec((D_MODEL, D_MODEL)),
            _const_spec((1, D_MODEL)),
            _const_spec((D_PLE, D_MODEL)),
        ],
        out_specs=pl.BlockSpec((POST_TOKENS, D_MODEL), tok),
        out_shape=jax.ShapeDtypeStruct((t, D_MODEL), F32),
        compiler_params=_params("arbitrary"),
        name="post_layer",
    )(h, yT, p, g2, b2, wg, bg, wp)


def _row(v):
    return v.reshape(1, -1).astype(F32)


def _sgu_spatial_params(w_s, b_s, dec_seq):
    tri = jnp.tril(jnp.ones((CHUNK, CHUNK), bool))
    ws_prompt = jnp.where(tri, w_s, 0.0)
    tri_s = jnp.tril(jnp.ones((dec_seq, dec_seq), bool))
    small = jnp.where(tri_s, w_s[:, :dec_seq, :dec_seq], 0.0)
    eye = jnp.eye(CHUNK // dec_seq, dtype=w_s.dtype)
    ws_sample = jnp.einsum("ab,gts->gatbs", eye, small).reshape(SGU_GROUPS, CHUNK, CHUNK)
    ws2 = jnp.stack([ws_prompt, ws_sample]).astype(BF16)
    bs_prompt = b_s.T
    bs_sample = jnp.tile(b_s[:, :dec_seq].T, (CHUNK // dec_seq, 1))
    bs2 = jnp.stack([bs_prompt, bs_sample])
    bs2 = jnp.repeat(bs2, SGU_GROUP_DIM, axis=2).astype(F32)
    return ws2, bs2


def kernel(x_prompt, x_sample, cache_k_win, cache_v_win, p_prompt, p_sample,
           ln1_g, ln1_b, ln2_g, ln2_b,
           sgu_w_in, sgu_b_in, sgu_ln_g, sgu_ln_b, sgu_w_s, sgu_b_s, sgu_w_out, sgu_b_out,
           attn_w_kv, attn_w_q, attn_sinks, attn_w_o,
           peer_w_q, peer_subkeys, peer_u, peer_v,
           ple_w, ple_gate_w, ple_gate_b):
    batch, seq, _ = x_prompt.shape
    dec_batch, dec_seq, _ = x_sample.shape
    tp = batch * seq
    ts = dec_batch * dec_seq
    half = N_KV_HEADS * HEAD_DIM
    wc = cache_k_win.shape[1]
    assert wc == WINDOW and CHUNK % dec_seq == 0 and seq % WINDOW == 0
    assert tp % PEER_TOKENS == 0 and ts % PEER_TOKENS == 0 and ts % SGU_TOKENS == 0

    h = jnp.concatenate([x_prompt.reshape(tp, D_MODEL), x_sample.reshape(ts, D_MODEL)], axis=0)
    p_all = jnp.concatenate([p_prompt.reshape(DEPTH, tp, D_PLE),
                             p_sample.reshape(DEPTH, ts, D_PLE)], axis=1)

    sgu_rows = []
    k_all = v_all = None
    for i in range(DEPTH):
        g1, b1 = _row(ln1_g[i]), _row(ln1_b[i])
        if i < N_A_LAYERS:
            ws2, bs2 = _sgu_spatial_params(sgu_w_s[i], sgu_b_s[i], dec_seq)
            h, v_rows = _sgu_layer(
                h, tp, sgu_w_in[i].astype(BF16), _row(sgu_b_in[i]), _row(sgu_ln_g[i]),
                _row(sgu_ln_b[i]), ws2, bs2, sgu_w_out[i].astype(BF16), _row(sgu_b_out[i]), g1, b1)
            sgu_rows.append(v_rows.reshape(dec_batch, dec_seq, D_SGU))
        else:
            if i == N_A_LAYERS:
                k_all, v_all = _kv_proj(h, attn_w_kv.astype(BF16))
                kp, vp = k_all[:tp], v_all[:tp]
                kn = k_all[tp:].reshape(dec_batch, dec_seq, half)
                vn = v_all[tp:].reshape(dec_batch, dec_seq, half)
                pad = ((0, 0), (0, WINDOW - dec_seq), (0, 0))
                kn_pad = jnp.pad(kn, pad).reshape(dec_batch * WINDOW, half)
                vn_pad = jnp.pad(vn, pad).reshape(dec_batch * WINDOW, half)
                ck = cache_k_win.reshape(dec_batch * wc, half).astype(F32)
                cv = cache_v_win.reshape(dec_batch * wc, half).astype(F32)
            jl = i - N_A_LAYERS
            wqT = attn_w_q[jl].T.astype(BF16)
            woT = attn_w_o[jl].T.astype(BF16)
            sinks = attn_sinks[jl].astype(F32)
            hp = _attn_layer(h[:tp], kp, kp, vp, vp, sinks, wqT, woT, g1, b1,
                             blocks_per_seq=seq // WINDOW)
            hs_pad = jnp.pad(h[tp:].reshape(dec_batch, dec_seq, D_MODEL),
                             ((0, 0), (0, WINDOW - dec_seq), (0, 0))).reshape(dec_batch * WINDOW, D_MODEL)
            hs = _attn_layer(hs_pad, ck, kn_pad, cv, vn_pad, sinks, wqT, woT, g1, b1,
                             blocks_per_seq=0)
            hs = hs.reshape(dec_batch, WINDOW, D_MODEL)[:, :dec_seq].reshape(ts, D_MODEL)
            h = jnp.concatenate([hp, hs], axis=0)
        y = _peer_layer(h, peer_w_q[i].T.astype(BF16), peer_subkeys[i].astype(BF16),
                        peer_u[i].astype(BF16),
                        peer_v[i].reshape(N_EXPERTS // PEER_EXPERTS, PEER_EXPERTS, D_MODEL)
                        .transpose(0, 2, 1).astype(BF16))
        h = _post_layer(h, y, p_all[i], _row(ln2_g[i]), _row(ln2_b[i]),
                        ple_gate_w[i].astype(BF16), _row(ple_gate_b[i]), ple_w[i].astype(BF16))

    y_prompt = h[:tp].reshape(batch, seq, D_MODEL)
    y_sample = h[tp:].reshape(dec_batch, dec_seq, D_MODEL)
    kp4 = kp.reshape(batch, seq, N_KV_HEADS, HEAD_DIM)
    vp4 = vp.reshape(batch, seq, N_KV_HEADS, HEAD_DIM)
    new_k_win_prompt = kp4[:, -WINDOW:]
    new_v_win_prompt = vp4[:, -WINDOW:]
    kn4 = kn.reshape(dec_batch, dec_seq, N_KV_HEADS, HEAD_DIM)
    vn4 = vn.reshape(dec_batch, dec_seq, N_KV_HEADS, HEAD_DIM)
    new_k_win_sample = jnp.concatenate([cache_k_win.astype(F32), kn4], axis=1)[:, -wc:]
    new_v_win_sample = jnp.concatenate([cache_v_win.astype(F32), vn4], axis=1)[:, -wc:]
    new_sgu_v_sample = jnp.stack(sgu_rows, axis=0)
    return (y_prompt, y_sample, new_k_win_prompt, new_v_win_prompt,
            new_k_win_sample, new_v_win_sample, new_sgu_v_sample)
```

```python
import functools

import jax
import jax.numpy as jnp
from jax import lax
from jax.experimental import pallas as pl
from jax.experimental.pallas import tpu as pltpu

D_MODEL = 1024
DEPTH = 4
N_A_LAYERS = DEPTH // 2
CHUNK = 128
D_SGU = 2 * D_MODEL
SGU_GROUPS = 8
SGU_GROUP_DIM = D_SGU // SGU_GROUPS
HEAD_DIM = 64
N_HEADS = D_MODEL // HEAD_DIM
N_KV_HEADS = 2
GROUP = N_HEADS // N_KV_HEADS
WINDOW = 128
PEER_HEADS = 8
N_KEYS = 128
N_EXPERTS = N_KEYS * N_KEYS
D_KEY = 256
PEER_TOPK = 16
D_PLE = 256
DEEPNORM_ALPHA = (2.0 * DEPTH) ** 0.25
LN_EPS = 1e-5

LANES = 128
MXU_DIM = 256
VMEM_LIMIT_BYTES = 56 * 1024 * 1024

SGU_TOKENS = 256
PEER_TOKENS = 512
PEER_EXPERTS = 1024
POST_TOKENS = 512
KV_TOKENS = 512

NEG_INF = float("-inf")
NT_DIMS = (((1,), (1,)), ((), ()))

BF16 = jnp.bfloat16
F32 = jnp.float32


def _layer_norm(x, g, b):
    mu = jnp.mean(x, axis=-1, keepdims=True)
    xc = x - mu
    var = jnp.mean(xc * xc, axis=-1, keepdims=True)
    return xc * lax.rsqrt(var + LN_EPS) * g + b


def _dot(a, b):
    return jnp.dot(a, b, preferred_element_type=F32)


def _dot_nt(a, b):
    return lax.dot_general(a, b, NT_DIMS, preferred_element_type=F32)


def _const_spec(shape):
    n = len(shape)
    return pl.BlockSpec(shape, lambda *_: (0,) * n)


def _params(*semantics):
    return pltpu.CompilerParams(dimension_semantics=semantics,
                                vmem_limit_bytes=VMEM_LIMIT_BYTES)


def _sgu_kernel(x_ref, win_ref, bin_ref, lng_ref, lnb_ref, ws_ref, bs_ref,
                wout_ref, bout_ref, g1_ref, b1_ref, h_ref, v_ref,
                u_scr, v_scr):
    x = x_ref[...]
    xb = x.astype(BF16)
    n_col = (2 * D_SGU) // 512
    for j in range(n_col):
        z = _dot(xb, win_ref[:, j * 512:(j + 1) * 512]) + bin_ref[:, j * 512:(j + 1) * 512]
        z = jax.nn.gelu(z)
        if j < n_col // 2:
            u_scr[:, j * 512:(j + 1) * 512] = z
        else:
            jj = j - n_col // 2
            v_scr[:, jj * 512:(jj + 1) * 512] = z
    v = _layer_norm(v_scr[...], lng_ref[...], lnb_ref[...])
    v_ref[...] = v
    v_scr[...] = v
    for c in range(SGU_TOKENS // CHUNK):
        rows = slice(c * CHUNK, (c + 1) * CHUNK)
        for g in range(SGU_GROUPS):
            cols = slice(g * SGU_GROUP_DIM, (g + 1) * SGU_GROUP_DIM)
            mixed = _dot(ws_ref[0, g], v_scr[rows, cols].astype(BF16)) + bs_ref[0, :, cols]
            u_scr[rows, cols] = u_scr[rows, cols] * mixed
    out = _dot(u_scr[...].astype(BF16), wout_ref[...]) + bout_ref[...]
    h_ref[...] = _layer_norm(DEEPNORM_ALPHA * x + out, g1_ref[...], b1_ref[...])


def _sgu_layer(h, n_prompt_tokens, win, b_in, ln_g, ln_b, ws2, bs2, wout, b_out, g1, b1):
    t = h.shape[0]
    n_tiles = t // SGU_TOKENS
    n_prompt_tiles = n_prompt_tokens // SGU_TOKENS
    t_sample = t - n_prompt_tokens

    def kind(i):
        return jnp.where(i < n_prompt_tiles, 0, 1)

    return pl.pallas_call(
        _sgu_kernel,
        grid=(n_tiles,),
        in_specs=[
            pl.BlockSpec((SGU_TOKENS, D_MODEL), lambda i: (i, 0)),
            _const_spec((D_MODEL, 2 * D_SGU)),
            _const_spec((1, 2 * D_SGU)),
            _const_spec((1, D_SGU)),
            _const_spec((1, D_SGU)),
            pl.BlockSpec((1, SGU_GROUPS, CHUNK, CHUNK), lambda i: (kind(i), 0, 0, 0)),
            pl.BlockSpec((1, CHUNK, D_SGU), lambda i: (kind(i), 0, 0)),
            _const_spec((D_SGU, D_MODEL)),
            _const_spec((1, D_MODEL)),
            _const_spec((1, D_MODEL)),
            _const_spec((1, D_MODEL)),
        ],
        out_specs=[
            pl.BlockSpec((SGU_TOKENS, D_MODEL), lambda i: (i, 0)),
            pl.BlockSpec((SGU_TOKENS, D_SGU), lambda i: (jnp.maximum(i - n_prompt_tiles, 0), 0)),
        ],
        out_shape=[
            jax.ShapeDtypeStruct((t, D_MODEL), F32),
            jax.ShapeDtypeStruct((t_sample, D_SGU), F32),
        ],
        scratch_shapes=[
            pltpu.VMEM((SGU_TOKENS, D_SGU), F32),
            pltpu.VMEM((SGU_TOKENS, D_SGU), F32),
        ],
        compiler_params=_params("arbitrary"),
        name="sgu_layer",
    )(h, win, b_in, ln_g, ln_b, ws2, bs2, wout, b_out, g1, b1)


def _kv_kernel(h_ref, w_ref, k_ref, v_ref):
    kv = _dot(h_ref[...].astype(BF16), w_ref[...])
    half = N_KV_HEADS * HEAD_DIM
    k_ref[...] = kv[:, :half]
    v_ref[...] = kv[:, half:]


def _kv_proj(h, w_kv):
    t = h.shape[0]
    half = N_KV_HEADS * HEAD_DIM
    return pl.pallas_call(
        _kv_kernel,
        grid=(t // KV_TOKENS,),
        in_specs=[
            pl.BlockSpec((KV_TOKENS, D_MODEL), lambda i: (i, 0)),
            _const_spec((D_MODEL, 2 * half)),
        ],
        out_specs=[
            pl.BlockSpec((KV_TOKENS, half), lambda i: (i, 0)),
            pl.BlockSpec((KV_TOKENS, half), lambda i: (i, 0)),
        ],
        out_shape=[jax.ShapeDtypeStruct((t, half), F32)] * 2,
        compiler_params=_params("arbitrary"),
        name="kv_proj",
    )(h, w_kv)


def _sink_softmax_rows(s, valid, sink):
    s = jnp.where(valid, s, NEG_INF)
    m = jnp.maximum(jnp.max(s, axis=0, keepdims=True), sink)
    p = jnp.exp(s - m)
    den = jnp.sum(p, axis=0, keepdims=True) + jnp.exp(sink - m)
    return p / den


def _alibi_slope(h):
    return 2.0 ** (-8.0 * (h + 1) / N_HEADS)


def _attn_prompt_kernel(sink_ref, h_ref, kp_ref, kc_ref, vp_ref, vc_ref, wqT_ref, woT_ref,
                        g1_ref, b1_ref, o_ref, oT_scr, *, blocks_per_seq):
    hb = h_ref[...]
    qT = _dot_nt(wqT_ref[...], hb.astype(BF16)).astype(BF16)
    kk = jnp.concatenate([kp_ref[...], kc_ref[...]], axis=0).astype(BF16)
    vv = jnp.concatenate([vp_ref[...], vc_ref[...]], axis=0)
    vvT = vv.T.astype(BF16)

    kj = lax.broadcasted_iota(jnp.int32, (2 * WINDOW, WINDOW), 0)
    qi = lax.broadcasted_iota(jnp.int32, (2 * WINDOW, WINDOW), 1)
    dist = qi + WINDOW - kj
    is_first = (pl.program_id(0) % blocks_per_seq == 0).astype(jnp.int32)
    valid = (dist >= 0) & (dist < WINDOW) & (kj >= is_first * WINDOW)
    distf = dist.astype(F32)

    zeros_q = jnp.zeros((HEAD_DIM, GROUP * WINDOW), BF16)
    for k in range(N_KV_HEADS):
        heads = range(k * GROUP, (k + 1) * GROUP)
        q_grp = jnp.concatenate([qT[h * HEAD_DIM:(h + 1) * HEAD_DIM] for h in heads], axis=1)
        q_pad = jnp.concatenate([q_grp, zeros_q] if k == 0 else [zeros_q, q_grp], axis=0)
        s_all = _dot(kk, q_pad) * (HEAD_DIM ** -0.5)
        p_blocks = []
        for g, h in enumerate(heads):
            s = s_all[:, g * WINDOW:(g + 1) * WINDOW] - _alibi_slope(h) * distf
            p_blocks.append(_sink_softmax_rows(s, valid, sink_ref[h]).astype(BF16))
        o_all = _dot(vvT[k * HEAD_DIM:(k + 1) * HEAD_DIM],
                     jnp.concatenate(p_blocks, axis=1))
        for g, h in enumerate(heads):
            oT_scr[h * HEAD_DIM:(h + 1) * HEAD_DIM, :] = o_all[:, g * WINDOW:(g + 1) * WINDOW]
    outT = _dot(woT_ref[...], oT_scr[...].astype(BF16))
    o_ref[...] = _layer_norm(DEEPNORM_ALPHA * hb + outT.T, g1_ref[...], b1_ref[...])


def _attn_prompt(h, k_all, v_all, sinks, wqT, woT, g1, b1, *, n_prompt_tokens, blocks_per_seq):
    t = h.shape[0]
    half = N_KV_HEADS * HEAD_DIM
    prev_map = lambda i: (jnp.maximum(i - 1, 0), 0)
    cur_map = lambda i: (i, 0)
    return pl.pallas_call(
        functools.partial(_attn_prompt_kernel, blocks_per_seq=blocks_per_seq),
        grid=(n_prompt_tokens // WINDOW,),
        in_specs=[
            pl.BlockSpec(memory_space=pltpu.SMEM),
            pl.BlockSpec((WINDOW, D_MODEL), cur_map),
            pl.BlockSpec((WINDOW, half), prev_map),
            pl.BlockSpec((WINDOW, half), cur_map),
            pl.BlockSpec((WINDOW, half), prev_map),
            pl.BlockSpec((WINDOW, half), cur_map),
            _const_spec((D_MODEL, D_MODEL)),
            _const_spec((D_MODEL, D_MODEL)),
            _const_spec((1, D_MODEL)),
            _const_spec((1, D_MODEL)),
        ],
        out_specs=pl.BlockSpec((WINDOW, D_MODEL), cur_map),
        out_shape=jax.ShapeDtypeStruct((t, D_MODEL), F32),
        scratch_shapes=[pltpu.VMEM((D_MODEL, WINDOW), F32)],
        compiler_params=_params("arbitrary"),
        name="attn_prompt",
    )(sinks, h, k_all, k_all, v_all, v_all, wqT, woT, g1, b1)


def _attn_sample_kernel(sink_ref, h_ref, ck_ref, kn_ref, cv_ref, vn_ref, wqT_ref, woT_ref,
                        g1_ref, b1_ref, _, o_ref, oT_scr, *, dec_seq):
    n_seq = WINDOW // dec_seq
    n_cache = n_seq * WINDOW
    n_keys = n_cache + WINDOW
    hb = h_ref[...]
    qT = _dot_nt(wqT_ref[...], hb.astype(BF16)).astype(BF16)
    kk = jnp.concatenate([ck_ref[...], kn_ref[...]], axis=0).astype(BF16)
    vv = jnp.concatenate([cv_ref[...], vn_ref[...]], axis=0)
    vvT = vv.T.astype(BF16)

    kidx = lax.broadcasted_iota(jnp.int32, (n_keys, WINDOW), 0)
    q = lax.broadcasted_iota(jnp.int32, (n_keys, WINDOW), 1)
    seq_shift, win_shift = dec_seq.bit_length() - 1, WINDOW.bit_length() - 1
    q_seq, q_pos = q >> seq_shift, q & (dec_seq - 1)
    is_cache = kidx < n_cache
    new_idx = jnp.maximum(kidx - n_cache, 0)
    k_seq = jnp.where(is_cache, kidx >> win_shift, new_idx >> seq_shift)
    dist = jnp.where(is_cache, q_pos + WINDOW - (kidx & (WINDOW - 1)),
                     q_pos - (new_idx & (dec_seq - 1)))
    valid = (k_seq == q_seq) & (dist >= 0) & (dist < WINDOW)
    distf = dist.astype(F32)

    zeros_q = jnp.zeros((HEAD_DIM, WINDOW), BF16)
    for h in range(N_HEADS):
        k = h // GROUP
        q_h = qT[h * HEAD_DIM:(h + 1) * HEAD_DIM]
        q_pad = jnp.concatenate([q_h, zeros_q] if k == 0 else [zeros_q, q_h], axis=0)
        s = _dot(kk, q_pad) * (HEAD_DIM ** -0.5) - _alibi_slope(h) * distf
        p = _sink_softmax_rows(s, valid, sink_ref[h]).astype(BF16)
        oT_scr[h * HEAD_DIM:(h + 1) * HEAD_DIM, :] = _dot(vvT[k * HEAD_DIM:(k + 1) * HEAD_DIM], p)
    outT = _dot(woT_ref[...], oT_scr[...].astype(BF16))
    o_ref[...] = _layer_norm(DEEPNORM_ALPHA * hb + outT.T, g1_ref[...], b1_ref[...])


def _attn_sample(h, k_all, v_all, cache_k, cache_v, sinks, wqT, woT, g1, b1, partial_out,
                 *, n_prompt_tokens, dec_seq):
    t = h.shape[0]
    half = N_KV_HEADS * HEAD_DIM
    first = n_prompt_tokens // WINDOW
    n_seq = WINDOW // dec_seq
    row_map = lambda i: (first + i, 0)
    cache_map = lambda i: (i, 0)
    return pl.pallas_call(
        functools.partial(_attn_sample_kernel, dec_seq=dec_seq),
        grid=((t - n_prompt_tokens) // WINDOW,),
        in_specs=[
            pl.BlockSpec(memory_space=pltpu.SMEM),
            pl.BlockSpec((WINDOW, D_MODEL), row_map),
            pl.BlockSpec((n_seq * WINDOW, half), cache_map),
            pl.BlockSpec((WINDOW, half), row_map),
            pl.BlockSpec((n_seq * WINDOW, half), cache_map),
            pl.BlockSpec((WINDOW, half), row_map),
            _const_spec((D_MODEL, D_MODEL)),
            _const_spec((D_MODEL, D_MODEL)),
            _const_spec((1, D_MODEL)),
            _const_spec((1, D_MODEL)),
            pl.BlockSpec(memory_space=pl.ANY),
        ],
        out_specs=pl.BlockSpec((WINDOW, D_MODEL), row_map),
        out_shape=jax.ShapeDtypeStruct((t, D_MODEL), F32),
        scratch_shapes=[pltpu.VMEM((D_MODEL, WINDOW), F32)],
        input_output_aliases={10: 0},
        compiler_params=_params("arbitrary"),
        name="attn_sample",
    )(sinks, h, cache_k, k_all, cache_v, v_all, wqT, woT, g1, b1, partial_out)


NOT_RANKED = 127.0


def _top16_rows(s):
    n = N_KEYS // 8
    v = [s[8 * i:8 * (i + 1)] for i in range(n)]
    k = 2
    while k <= n:
        j = k // 2
        while j >= 1:
            for i in range(n):
                l = i ^ j
                if l > i:
                    hi, lo = jnp.maximum(v[i], v[l]), jnp.minimum(v[i], v[l])
                    v[i], v[l] = (hi, lo) if (i & k) == 0 else (lo, hi)
            j //= 2
        k *= 2
    row_id = lax.broadcasted_iota(jnp.int32, (PEER_TOPK, LANES), 0)
    rows = []
    stacked = jnp.full((PEER_TOPK, LANES), NEG_INF, F32)
    for i in range(PEER_TOPK):
        m = jnp.max(v[0], axis=0, keepdims=True)
        rows.append(m)
        stacked = jnp.where(row_id == i, m, stacked)
        hit = v[0] == m
        for d in range(PEER_TOPK - 1 - i):
            v[d] = jnp.where(hit, v[d + 1], v[d])
    return rows, stacked


def _rank_of(s, rows):
    rank = jnp.full(s.shape, NOT_RANKED, F32)
    for i in reversed(range(len(rows))):
        rank = jnp.where(s >= rows[i], float(i), rank)
    return rank


def _pair_threshold(r1, t1, r2, t2):
    row8 = lax.broadcasted_iota(jnp.int32, (8, LANES), 0)
    cands = []
    for i in range(8):
        n = PEER_TOPK // (i + 1)
        c = r1[i] + t2[0:8]
        if n < 8:
            c = jnp.where(row8 < n, c, NEG_INF)
        cands.append(c)
    cands.append(r1[0] + t2[8:16])
    cands.append(t1[8:16] + r2[0])
    top = r1[0] + r2[0]
    z = jnp.zeros((1, LANES), F32)
    m = top
    for _ in range(PEER_TOPK):
        m = cands[0]
        for c in cands[1:]:
            m = jnp.maximum(m, c)
        m = jnp.max(m, axis=0, keepdims=True)
        z = z + jnp.exp(m - top)
        cands = [jnp.where(c == m, NEG_INF, c) for c in cands]
    return m, z


def _peer_select(x_ref, wqT_ref, keys_ref, xb_scr, qT_scr, s1_scr, s2_scr,
                 cnt_scr, e1_scr, rank2_scr, e2_scr):
    xb_scr[...] = x_ref[...].T.astype(BF16)
    qT_scr[...] = _dot(wqT_ref[...], xb_scr[...]).astype(BF16)
    for h in range(PEER_HEADS):
        for p, dst in ((0, s1_scr), (1, s2_scr)):
            r0 = (h * 2 + p) * (D_KEY // 2)
            dst[h] = _dot(keys_ref[p], qT_scr[r0:r0 + D_KEY // 2, :])

    def chunk_body(c, carry):
        lanes = pl.ds(pl.multiple_of(c * LANES, LANES), LANES)

        def one_head(h):
            s1 = s1_scr[h, :, lanes]
            s2 = s2_scr[h, :, lanes]
            r1, t1 = _top16_rows(s1)
            r2, t2 = _top16_rows(s2)
            tau, z = _pair_threshold(r1, t1, r2, t2)
            cnt_by_rank = jnp.zeros(t1.shape, F32)
            for jx in range(PEER_TOPK):
                cnt_by_rank = cnt_by_rank + jnp.where(t1 + r2[jx] >= tau, 1.0, 0.0)
            cnt = jnp.zeros(s1.shape, F32)
            for i in reversed(range(PEER_TOPK)):
                cnt = jnp.where(s1 >= r1[i], cnt_by_rank[i:i + 1], cnt)
            cnt_scr[h, :, lanes] = cnt
            e1_scr[h, :, lanes] = jnp.exp(s1 - r1[0]) * (1.0 / z)
            rank2_scr[h, :, lanes] = _rank_of(s2, r2).astype(BF16)
            e2_scr[h, :, lanes] = jnp.exp(s2 - r2[0]).astype(BF16)

        def head_pair_body(hp, carry2):
            one_head(2 * hp)
            one_head(2 * hp + 1)
            return carry2

        return lax.fori_loop(0, PEER_HEADS // 2, head_pair_body, carry)

    lax.fori_loop(0, PEER_TOKENS // LANES, chunk_body, 0)


def _peer_gate(j, cnt_scr, e1_scr, rank2_scr, e2_scr, w_scr):
    a_per_step = PEER_EXPERTS // N_KEYS
    a_rows = pl.ds(pl.multiple_of(j * a_per_step, a_per_step), a_per_step)
    zero = jnp.zeros((), BF16)
    for c in range(PEER_TOKENS // LANES):
        lanes = slice(c * LANES, (c + 1) * LANES)
        cnt_blk = [cnt_scr[h, a_rows, lanes] for h in range(PEER_HEADS)]
        e1_blk = [e1_scr[h, a_rows, lanes] for h in range(PEER_HEADS)]
        for al in range(a_per_step):
            w = None
            for h in range(PEER_HEADS):
                cnt = jnp.broadcast_to(cnt_blk[h][al:al + 1], (N_KEYS, LANES)).astype(BF16)
                e1 = jnp.broadcast_to(e1_blk[h][al:al + 1], (N_KEYS, LANES)).astype(BF16)
                wh = jnp.where(rank2_scr[h, :, lanes] < cnt, e2_scr[h, :, lanes] * e1, zero)
                w = wh if w is None else w + wh
            w_scr[al * N_KEYS:(al + 1) * N_KEYS, lanes] = w


def _peer_kernel(x_ref, wqT_ref, keys_ref, u_ref, vT_ref, o_ref,
                 xb_scr, qT_scr, s1_scr, s2_scr, cnt_scr, e1_scr, rank2_scr, e2_scr,
                 w_scr, hid_scr, acc_scr):
    j = pl.program_id(1)

    @pl.when(j == 0)
    def _start():
        _peer_select(x_ref, wqT_ref, keys_ref, xb_scr, qT_scr, s1_scr, s2_scr,
                     cnt_scr, e1_scr, rank2_scr, e2_scr)
        acc_scr[...] = jnp.zeros_like(acc_scr)

    hid_scr[...] = _dot(u_ref[...], xb_scr[...]).astype(BF16)
    _peer_gate(j, cnt_scr, e1_scr, rank2_scr, e2_scr, w_scr)
    total = None
    for k in range(PEER_EXPERTS // MXU_DIM):
        rows = slice(k * MXU_DIM, (k + 1) * MXU_DIM)
        g = jax.nn.gelu(hid_scr[rows, :]) * w_scr[rows, :]
        part = _dot(vT_ref[:, rows], g)
        total = part if total is None else total + part
    acc_scr[...] += total

    @pl.when(j == pl.num_programs(1) - 1)
    def _finish():
        o_ref[...] = acc_scr[...].T


def _peer_layer(h, wqT, keys, u_tab, vT_tab):
    t = h.shape[0]
    n_tiles = N_EXPERTS // PEER_EXPERTS
    return pl.pallas_call(
        _peer_kernel,
        grid=(t // PEER_TOKENS, n_tiles),
        in_specs=[
            pl.BlockSpec((PEER_TOKENS, D_MODEL), lambda i, j: (i, 0)),
            _const_spec((PEER_HEADS * D_KEY, D_MODEL)),
            _const_spec((2, N_KEYS, D_KEY // 2)),
            pl.BlockSpec((PEER_EXPERTS, D_MODEL), lambda i, j: (j, 0)),
            pl.BlockSpec((D_MODEL, PEER_EXPERTS), lambda i, j: (0, j)),
        ],
        out_specs=pl.BlockSpec((PEER_TOKENS, D_MODEL), lambda i, j: (i, 0)),
        out_shape=jax.ShapeDtypeStruct((t, D_MODEL), F32),
        scratch_shapes=[
            pltpu.VMEM((D_MODEL, PEER_TOKENS), BF16),
            pltpu.VMEM((PEER_HEADS * D_KEY, PEER_TOKENS), BF16),
            pltpu.VMEM((PEER_HEADS, N_KEYS, PEER_TOKENS), F32),
            pltpu.VMEM((PEER_HEADS, N_KEYS, PEER_TOKENS), F32),
            pltpu.VMEM((PEER_HEADS, N_KEYS, PEER_TOKENS), F32),
            pltpu.VMEM((PEER_HEADS, N_KEYS, PEER_TOKENS), F32),
            pltpu.VMEM((PEER_HEADS, N_KEYS, PEER_TOKENS), BF16),
            pltpu.VMEM((PEER_HEADS, N_KEYS, PEER_TOKENS), BF16),
            pltpu.VMEM((PEER_EXPERTS, PEER_TOKENS), BF16),
            pltpu.VMEM((PEER_EXPERTS, PEER_TOKENS), BF16),
            pltpu.VMEM((D_MODEL, PEER_TOKENS), F32),
        ],
        compiler_params=_params("arbitrary", "arbitrary"),
        name="peer_layer",
    )(h, wqT, keys, u_tab, vT_tab)


def _post_kernel(h_ref, y_ref, p_ref, g2_ref, b2_ref, wg_ref, bg_ref, wp_ref, o_ref):
    hn = _layer_norm(DEEPNORM_ALPHA * h_ref[...] + y_ref[...], g2_ref[...], b2_ref[...])
    gate = jax.nn.sigmoid(_dot(hn.astype(BF16), wg_ref[...]) + bg_ref[...])
    o_ref[...] = hn + gate * _dot(p_ref[...].astype(BF16), wp_ref[...])


def _post_layer(h, y, p_prompt, p_sample, g2, b2, wg, bg, wp):
    t = h.shape[0]
    n_prompt_tiles = p_prompt.shape[0] // POST_TOKENS
    tok = lambda i: (i, 0)
    prompt_map = lambda i: (jnp.minimum(i, n_prompt_tiles - 1), 0)
    sample_map = lambda i: (jnp.maximum(i - n_prompt_tiles, 0), 0)

    def body(h_ref, y_ref, pp_ref, ps_ref, *rest):
        is_prompt = pl.program_id(0) < n_prompt_tiles

        @pl.when(is_prompt)
        def _():
            _post_kernel(h_ref, y_ref, pp_ref, *rest)

        @pl.when(jnp.logical_not(is_prompt))
        def _():
            _post_kernel(h_ref, y_ref, ps_ref, *rest)

    return pl.pallas_call(
        body,
        grid=(t // POST_TOKENS,),
        in_specs=[
            pl.BlockSpec((POST_TOKENS, D_MODEL), tok),
            pl.BlockSpec((POST_TOKENS, D_MODEL), tok),
            pl.BlockSpec((POST_TOKENS, D_PLE), prompt_map),
            pl.BlockSpec((POST_TOKENS, D_PLE), sample_map),
            _const_spec((1, D_MODEL)),
            _const_spec((1, D_MODEL)),
            _const_spec((D_MODEL, D_MODEL)),
            _const_spec((1, D_MODEL)),
            _const_spec((D_PLE, D_MODEL)),
        ],
        out_specs=pl.BlockSpec((POST_TOKENS, D_MODEL), tok),
        out_shape=jax.ShapeDtypeStruct((t, D_MODEL), F32),
        compiler_params=_params("arbitrary"),
        name="post_layer",
    )(h, y, p_prompt, p_sample, g2, b2, wg, bg, wp)


def _row(v):
    return v.reshape(1, -1).astype(F32)


def _sgu_spatial_params(w_s, b_s, dec_seq):
    tri = jnp.tril(jnp.ones((CHUNK, CHUNK), bool))
    ws_prompt = jnp.where(tri, w_s, 0.0)
    tri_s = jnp.tril(jnp.ones((dec_seq, dec_seq), bool))
    small = jnp.where(tri_s, w_s[:, :dec_seq, :dec_seq], 0.0)
    eye = jnp.eye(CHUNK // dec_seq, dtype=w_s.dtype)
    ws_sample = jnp.einsum("ab,gts->gatbs", eye, small).reshape(SGU_GROUPS, CHUNK, CHUNK)
    ws2 = jnp.stack([ws_prompt, ws_sample]).astype(BF16)
    bs_prompt = b_s.T
    bs_sample = jnp.tile(b_s[:, :dec_seq].T, (CHUNK // dec_seq, 1))
    bs2 = jnp.stack([bs_prompt, bs_sample])
    bs2 = jnp.repeat(bs2, SGU_GROUP_DIM, axis=2).astype(F32)
    return ws2, bs2


def kernel(x_prompt, x_sample, cache_k_win, cache_v_win, p_prompt, p_sample,
           ln1_g, ln1_b, ln2_g, ln2_b,
           sgu_w_in, sgu_b_in, sgu_ln_g, sgu_ln_b, sgu_w_s, sgu_b_s, sgu_w_out, sgu_b_out,
           attn_w_kv, attn_w_q, attn_sinks, attn_w_o,
           peer_w_q, peer_subkeys, peer_u, peer_v,
           ple_w, ple_gate_w, ple_gate_b):
    batch, seq, _ = x_prompt.shape
    dec_batch, dec_seq, _ = x_sample.shape
    tp = batch * seq
    ts = dec_batch * dec_seq
    half = N_KV_HEADS * HEAD_DIM
    wc = cache_k_win.shape[1]
    assert wc == WINDOW and CHUNK % dec_seq == 0 and seq % WINDOW == 0
    assert dec_seq & (dec_seq - 1) == 0 and WINDOW & (WINDOW - 1) == 0
    assert tp % PEER_TOKENS == 0 and ts % PEER_TOKENS == 0 and ts % SGU_TOKENS == 0

    h = jnp.concatenate([x_prompt.reshape(tp, D_MODEL), x_sample.reshape(ts, D_MODEL)], axis=0)
    pp = p_prompt.reshape(DEPTH, tp, D_PLE)
    ps = p_sample.reshape(DEPTH, ts, D_PLE)
    ck = cache_k_win.reshape(dec_batch * wc, half).astype(F32)
    cv = cache_v_win.reshape(dec_batch * wc, half).astype(F32)

    sgu_rows = []
    k_all = v_all = None
    for i in range(DEPTH):
        g1, b1 = _row(ln1_g[i]), _row(ln1_b[i])
        if i < N_A_LAYERS:
            ws2, bs2 = _sgu_spatial_params(sgu_w_s[i], sgu_b_s[i], dec_seq)
            h, v_rows = _sgu_layer(
                h, tp, sgu_w_in[i].astype(BF16), _row(sgu_b_in[i]), _row(sgu_ln_g[i]),
                _row(sgu_ln_b[i]), ws2, bs2, sgu_w_out[i].astype(BF16), _row(sgu_b_out[i]), g1, b1)
            sgu_rows.append(v_rows.reshape(dec_batch, dec_seq, D_SGU))
        else:
            if i == N_A_LAYERS:
                k_all, v_all = _kv_proj(h, attn_w_kv.astype(BF16))
            jl = i - N_A_LAYERS
            wqT = attn_w_q[jl].T.astype(BF16)
            woT = attn_w_o[jl].T.astype(BF16)
            sinks = attn_sinks[jl].astype(F32)
            part = _attn_prompt(h, k_all, v_all, sinks, wqT, woT, g1, b1,
                                n_prompt_tokens=tp, blocks_per_seq=seq // WINDOW)
            h = _attn_sample(h, k_all, v_all, ck, cv, sinks, wqT, woT, g1, b1, part,
                             n_prompt_tokens=tp, dec_seq=dec_seq)
        y = _peer_layer(h, peer_w_q[i].T.astype(BF16), peer_subkeys[i].astype(BF16),
                        peer_u[i].astype(BF16), peer_v[i].T.astype(BF16))
        h = _post_layer(h, y, pp[i], ps[i], _row(ln2_g[i]), _row(ln2_b[i]),
                        ple_gate_w[i].astype(BF16), _row(ple_gate_b[i]), ple_w[i].astype(BF16))

    y_prompt = h[:tp].reshape(batch, seq, D_MODEL)
    y_sample = h[tp:].reshape(dec_batch, dec_seq, D_MODEL)
    kp4 = k_all[:tp].reshape(batch, seq, N_KV_HEADS, HEAD_DIM)
    vp4 = v_all[:tp].reshape(batch, seq, N_KV_HEADS, HEAD_DIM)
    new_k_win_prompt = kp4[:, -WINDOW:]
    new_v_win_prompt = vp4[:, -WINDOW:]
    kn4 = k_all[tp:].reshape(dec_batch, dec_seq, N_KV_HEADS, HEAD_DIM)
    vn4 = v_all[tp:].reshape(dec_batch, dec_seq, N_KV_HEADS, HEAD_DIM)
    new_k_win_sample = jnp.concatenate([cache_k_win.astype(F32), kn4], axis=1)[:, -wc:]
    new_v_win_sample = jnp.concatenate([cache_v_win.astype(F32), vn4], axis=1)[:, -wc:]
    new_sgu_v_sample = jnp.stack(sgu_rows, axis=0)
    return (y_prompt, y_sample, new_k_win_prompt, new_v_win_prompt,
            new_k_win_sample, new_v_win_sample, new_sgu_v_sample)
```

```python
import functools

import jax
import jax.numpy as jnp
from jax import lax
from jax.experimental import pallas as pl
from jax.experimental.pallas import tpu as pltpu

D_MODEL = 1024
DEPTH = 4
N_A_LAYERS = DEPTH // 2
CHUNK = 128
D_SGU = 2 * D_MODEL
SGU_GROUPS = 8
SGU_GROUP_DIM = D_SGU // SGU_GROUPS
HEAD_DIM = 64
N_HEADS = D_MODEL // HEAD_DIM
N_KV_HEADS = 2
GROUP = N_HEADS // N_KV_HEADS
WINDOW = 128
PEER_HEADS = 8
N_KEYS = 128
N_EXPERTS = N_KEYS * N_KEYS
D_KEY = 256
PEER_TOPK = 16
D_PLE = 256
DEEPNORM_ALPHA = (2.0 * DEPTH) ** 0.25
LN_EPS = 1e-5

LANES = 128
MXU_DIM = 256
VMEM_LIMIT_BYTES = 56 * 1024 * 1024

SGU_TOKENS = 256
PEER_TOKENS = 512
PEER_EXPERTS = 1024
POST_TOKENS = 512
KV_TOKENS = 512

NEG_INF = float("-inf")
NT_DIMS = (((1,), (1,)), ((), ()))

BF16 = jnp.bfloat16
F32 = jnp.float32


def _layer_norm(x, g, b):
    mu = jnp.mean(x, axis=-1, keepdims=True)
    xc = x - mu
    var = jnp.mean(xc * xc, axis=-1, keepdims=True)
    return xc * lax.rsqrt(var + LN_EPS) * g + b


def _dot(a, b):
    return jnp.dot(a, b, preferred_element_type=F32)


def _dot_nt(a, b):
    return lax.dot_general(a, b, NT_DIMS, preferred_element_type=F32)


def _const_spec(shape):
    n = len(shape)
    return pl.BlockSpec(shape, lambda *_: (0,) * n)


def _params(*semantics):
    return pltpu.CompilerParams(dimension_semantics=semantics,
                                vmem_limit_bytes=VMEM_LIMIT_BYTES)


def _sgu_kernel(x_ref, win_ref, bin_ref, lng_ref, lnb_ref, ws_ref, bs_ref,
                wout_ref, bout_ref, g1_ref, b1_ref, h_ref, v_ref,
                u_scr, v_scr):
    x = x_ref[...]
    xb = x.astype(BF16)
    n_col = (2 * D_SGU) // 512
    for j in range(n_col):
        z = _dot(xb, win_ref[:, j * 512:(j + 1) * 512]) + bin_ref[:, j * 512:(j + 1) * 512]
        z = jax.nn.gelu(z)
        if j < n_col // 2:
            u_scr[:, j * 512:(j + 1) * 512] = z
        else:
            jj = j - n_col // 2
            v_scr[:, jj * 512:(jj + 1) * 512] = z
    v = _layer_norm(v_scr[...], lng_ref[...], lnb_ref[...])
    v_ref[...] = v
    v_scr[...] = v
    for c in range(SGU_TOKENS // CHUNK):
        rows = slice(c * CHUNK, (c + 1) * CHUNK)
        for g in range(SGU_GROUPS):
            cols = slice(g * SGU_GROUP_DIM, (g + 1) * SGU_GROUP_DIM)
            mixed = _dot(ws_ref[0, g], v_scr[rows, cols].astype(BF16)) + bs_ref[0, :, cols]
            u_scr[rows, cols] = u_scr[rows, cols] * mixed
    out = _dot(u_scr[...].astype(BF16), wout_ref[...]) + bout_ref[...]
    h_ref[...] = _layer_norm(DEEPNORM_ALPHA * x + out, g1_ref[...], b1_ref[...])


def _sgu_layer(h, n_prompt_tokens, win, b_in, ln_g, ln_b, ws2, bs2, wout, b_out, g1, b1):
    t = h.shape[0]
    n_tiles = t // SGU_TOKENS
    n_prompt_tiles = n_prompt_tokens // SGU_TOKENS
    t_sample = t - n_prompt_tokens

    def kind(i):
        return jnp.where(i < n_prompt_tiles, 0, 1)

    return pl.pallas_call(
        _sgu_kernel,
        grid=(n_tiles,),
        in_specs=[
            pl.BlockSpec((SGU_TOKENS, D_MODEL), lambda i: (i, 0)),
            _const_spec((D_MODEL, 2 * D_SGU)),
            _const_spec((1, 2 * D_SGU)),
            _const_spec((1, D_SGU)),
            _const_spec((1, D_SGU)),
            pl.BlockSpec((1, SGU_GROUPS, CHUNK, CHUNK), lambda i: (kind(i), 0, 0, 0)),
            pl.BlockSpec((1, CHUNK, D_SGU), lambda i: (kind(i), 0, 0)),
            _const_spec((D_SGU, D_MODEL)),
            _const_spec((1, D_MODEL)),
            _const_spec((1, D_MODEL)),
            _const_spec((1, D_MODEL)),
        ],
        out_specs=[
            pl.BlockSpec((SGU_TOKENS, D_MODEL), lambda i: (i, 0)),
            pl.BlockSpec((SGU_TOKENS, D_SGU), lambda i: (jnp.maximum(i - n_prompt_tiles, 0), 0)),
        ],
        out_shape=[
            jax.ShapeDtypeStruct((t, D_MODEL), F32),
            jax.ShapeDtypeStruct((t_sample, D_SGU), F32),
        ],
        scratch_shapes=[
            pltpu.VMEM((SGU_TOKENS, D_SGU), F32),
            pltpu.VMEM((SGU_TOKENS, D_SGU), F32),
        ],
        compiler_params=_params("arbitrary"),
        name="sgu_layer",
    )(h, win, b_in, ln_g, ln_b, ws2, bs2, wout, b_out, g1, b1)


def _kv_kernel(h_ref, w_ref, k_ref, v_ref):
    kv = _dot(h_ref[...].astype(BF16), w_ref[...])
    half = N_KV_HEADS * HEAD_DIM
    k_ref[...] = kv[:, :half]
    v_ref[...] = kv[:, half:]


def _kv_proj(h, w_kv):
    t = h.shape[0]
    half = N_KV_HEADS * HEAD_DIM
    return pl.pallas_call(
        _kv_kernel,
        grid=(t // KV_TOKENS,),
        in_specs=[
            pl.BlockSpec((KV_TOKENS, D_MODEL), lambda i: (i, 0)),
            _const_spec((D_MODEL, 2 * half)),
        ],
        out_specs=[
            pl.BlockSpec((KV_TOKENS, half), lambda i: (i, 0)),
            pl.BlockSpec((KV_TOKENS, half), lambda i: (i, 0)),
        ],
        out_shape=[jax.ShapeDtypeStruct((t, half), F32)] * 2,
        compiler_params=_params("arbitrary"),
        name="kv_proj",
    )(h, w_kv)


def _sink_softmax_rows(s, valid, sink):
    s = jnp.where(valid, s, NEG_INF)
    m = jnp.maximum(jnp.max(s, axis=0, keepdims=True), sink)
    p = jnp.exp(s - m)
    den = jnp.sum(p, axis=0, keepdims=True) + jnp.exp(sink - m)
    return p / den


def _alibi_slope(h):
    return 2.0 ** (-8.0 * (h + 1) / N_HEADS)


def _attn_prompt_kernel(sink_ref, h_ref, kp_ref, kc_ref, vp_ref, vc_ref, wqT_ref, woT_ref,
                        g1_ref, b1_ref, o_ref, oT_scr, *, blocks_per_seq):
    hb = h_ref[...]
    qT = _dot_nt(wqT_ref[...], hb.astype(BF16)).astype(BF16)
    kk = jnp.concatenate([kp_ref[...], kc_ref[...]], axis=0).astype(BF16)
    vv = jnp.concatenate([vp_ref[...], vc_ref[...]], axis=0)
    vvT = vv.T.astype(BF16)

    kj = lax.broadcasted_iota(jnp.int32, (2 * WINDOW, WINDOW), 0)
    qi = lax.broadcasted_iota(jnp.int32, (2 * WINDOW, WINDOW), 1)
    dist = qi + WINDOW - kj
    is_first = (pl.program_id(0) % blocks_per_seq == 0).astype(jnp.int32)
    valid = (dist >= 0) & (dist < WINDOW) & (kj >= is_first * WINDOW)
    distf = dist.astype(F32)

    zeros_q = jnp.zeros((HEAD_DIM, GROUP * WINDOW), BF16)
    for k in range(N_KV_HEADS):
        heads = range(k * GROUP, (k + 1) * GROUP)
        q_grp = jnp.concatenate([qT[h * HEAD_DIM:(h + 1) * HEAD_DIM] for h in heads], axis=1)
        q_pad = jnp.concatenate([q_grp, zeros_q] if k == 0 else [zeros_q, q_grp], axis=0)
        s_all = _dot(kk, q_pad) * (HEAD_DIM ** -0.5)
        p_blocks = []
        for g, h in enumerate(heads):
            s = s_all[:, g * WINDOW:(g + 1) * WINDOW] - _alibi_slope(h) * distf
            p_blocks.append(_sink_softmax_rows(s, valid, sink_ref[h]).astype(BF16))
        o_all = _dot(vvT[k * HEAD_DIM:(k + 1) * HEAD_DIM],
                     jnp.concatenate(p_blocks, axis=1))
        for g, h in enumerate(heads):
            oT_scr[h * HEAD_DIM:(h + 1) * HEAD_DIM, :] = o_all[:, g * WINDOW:(g + 1) * WINDOW]
    outT = _dot(woT_ref[...], oT_scr[...].astype(BF16))
    o_ref[...] = _layer_norm(DEEPNORM_ALPHA * hb + outT.T, g1_ref[...], b1_ref[...])


def _attn_sample_kernel(sink_ref, h_ref, ck_ref, kn_ref, cv_ref, vn_ref, wqT_ref, woT_ref,
                        g1_ref, b1_ref, o_ref, oT_scr, *, dec_seq):
    n_seq = WINDOW // dec_seq
    n_cache = n_seq * WINDOW
    n_keys = n_cache + WINDOW
    hb = h_ref[...]
    qT = _dot_nt(wqT_ref[...], hb.astype(BF16)).astype(BF16)
    kk = jnp.concatenate([ck_ref[...], kn_ref[...]], axis=0).astype(BF16)
    vv = jnp.concatenate([cv_ref[...], vn_ref[...]], axis=0)
    vvT = vv.T.astype(BF16)

    kidx = lax.broadcasted_iota(jnp.int32, (n_keys, WINDOW), 0)
    q = lax.broadcasted_iota(jnp.int32, (n_keys, WINDOW), 1)
    seq_shift, win_shift = dec_seq.bit_length() - 1, WINDOW.bit_length() - 1
    q_seq, q_pos = q >> seq_shift, q & (dec_seq - 1)
    is_cache = kidx < n_cache
    new_idx = jnp.maximum(kidx - n_cache, 0)
    k_seq = jnp.where(is_cache, kidx >> win_shift, new_idx >> seq_shift)
    dist = jnp.where(is_cache, q_pos + WINDOW - (kidx & (WINDOW - 1)),
                     q_pos - (new_idx & (dec_seq - 1)))
    valid = (k_seq == q_seq) & (dist >= 0) & (dist < WINDOW)
    distf = dist.astype(F32)

    zeros_q = jnp.zeros((HEAD_DIM, WINDOW), BF16)
    for h in range(N_HEADS):
        k = h // GROUP
        q_h = qT[h * HEAD_DIM:(h + 1) * HEAD_DIM]
        q_pad = jnp.concatenate([q_h, zeros_q] if k == 0 else [zeros_q, q_h], axis=0)
        s = _dot(kk, q_pad) * (HEAD_DIM ** -0.5) - _alibi_slope(h) * distf
        p = _sink_softmax_rows(s, valid, sink_ref[h]).astype(BF16)
        oT_scr[h * HEAD_DIM:(h + 1) * HEAD_DIM, :] = _dot(vvT[k * HEAD_DIM:(k + 1) * HEAD_DIM], p)
    outT = _dot(woT_ref[...], oT_scr[...].astype(BF16))
    o_ref[...] = _layer_norm(DEEPNORM_ALPHA * hb + outT.T, g1_ref[...], b1_ref[...])


def _attn_layer(h, k_all, v_all, cache_k, cache_v, sinks, wqT, woT, g1, b1,
                *, n_prompt_tokens, blocks_per_seq, dec_seq):
    t = h.shape[0]
    half = N_KV_HEADS * HEAD_DIM
    n_prompt_blocks = n_prompt_tokens // WINDOW
    n_sample_blocks = (t - n_prompt_tokens) // WINDOW
    n_seq = WINDOW // dec_seq
    cur_map = lambda i: (i, 0)
    prev_map = lambda i: (jnp.maximum(i - 1, 0), 0)
    cache_map = lambda i: (jnp.clip(i - n_prompt_blocks, 0, n_sample_blocks - 1), 0)

    def body(sink_ref, h_ref, kp_ref, kc_ref, vp_ref, vc_ref, ck_ref, cv_ref, *rest):
        is_prompt = pl.program_id(0) < n_prompt_blocks

        @pl.when(is_prompt)
        def _():
            _attn_prompt_kernel(sink_ref, h_ref, kp_ref, kc_ref, vp_ref, vc_ref, *rest,
                                blocks_per_seq=blocks_per_seq)

        @pl.when(jnp.logical_not(is_prompt))
        def _():
            _attn_sample_kernel(sink_ref, h_ref, ck_ref, kc_ref, cv_ref, vc_ref, *rest,
                                dec_seq=dec_seq)

    return pl.pallas_call(
        body,
        grid=(n_prompt_blocks + n_sample_blocks,),
        in_specs=[
            pl.BlockSpec(memory_space=pltpu.SMEM),
            pl.BlockSpec((WINDOW, D_MODEL), cur_map),
            pl.BlockSpec((WINDOW, half), prev_map),
            pl.BlockSpec((WINDOW, half), cur_map),
            pl.BlockSpec((WINDOW, half), prev_map),
            pl.BlockSpec((WINDOW, half), cur_map),
            pl.BlockSpec((n_seq * WINDOW, half), cache_map),
            pl.BlockSpec((n_seq * WINDOW, half), cache_map),
            _const_spec((D_MODEL, D_MODEL)),
            _const_spec((D_MODEL, D_MODEL)),
            _const_spec((1, D_MODEL)),
            _const_spec((1, D_MODEL)),
        ],
        out_specs=pl.BlockSpec((WINDOW, D_MODEL), cur_map),
        out_shape=jax.ShapeDtypeStruct((t, D_MODEL), F32),
        scratch_shapes=[pltpu.VMEM((D_MODEL, WINDOW), F32)],
        compiler_params=_params("arbitrary"),
        name="attn_layer",
    )(sinks, h, k_all, k_all, v_all, v_all, cache_k, cache_v, wqT, woT, g1, b1)


NOT_RANKED = 127.0


def _top16_rows(s):
    n = N_KEYS // 8
    v = [s[8 * i:8 * (i + 1)] for i in range(n)]
    k = 2
    while k <= n:
        j = k // 2
        while j >= 1:
            for i in range(n):
                l = i ^ j
                if l > i:
                    hi, lo = jnp.maximum(v[i], v[l]), jnp.minimum(v[i], v[l])
                    v[i], v[l] = (hi, lo) if (i & k) == 0 else (lo, hi)
            j //= 2
        k *= 2
    row_id = lax.broadcasted_iota(jnp.int32, (PEER_TOPK, LANES), 0)
    rows = []
    stacked = jnp.full((PEER_TOPK, LANES), NEG_INF, F32)
    for i in range(PEER_TOPK):
        m = jnp.max(v[0], axis=0, keepdims=True)
        rows.append(m)
        stacked = jnp.where(row_id == i, m, stacked)
        hit = v[0] == m
        for d in range(PEER_TOPK - 1 - i):
            v[d] = jnp.where(hit, v[d + 1], v[d])
    return rows, stacked


def _rank_of(s, rows):
    rank = jnp.full(s.shape, NOT_RANKED, F32)
    for i in reversed(range(len(rows))):
        rank = jnp.where(s >= rows[i], float(i), rank)
    return rank


def _pair_threshold(r1, t1, r2, t2):
    row8 = lax.broadcasted_iota(jnp.int32, (8, LANES), 0)
    cands = []
    for i in range(8):
        n = PEER_TOPK // (i + 1)
        c = r1[i] + t2[0:8]
        if n < 8:
            c = jnp.where(row8 < n, c, NEG_INF)
        cands.append(c)
    cands.append(r1[0] + t2[8:16])
    cands.append(t1[8:16] + r2[0])
    top = r1[0] + r2[0]
    z = jnp.zeros((1, LANES), F32)
    m = top
    for _ in range(PEER_TOPK):
        m = cands[0]
        for c in cands[1:]:
            m = jnp.maximum(m, c)
        m = jnp.max(m, axis=0, keepdims=True)
        z = z + jnp.exp(m - top)
        cands = [jnp.where(c == m, NEG_INF, c) for c in cands]
    return m, z


def _peer_select(x_ref, wqT_ref, keys_ref, xb_scr, qT_scr, s1_scr, s2_scr,
                 cnt_scr, e1_scr, rank2_scr, e2_scr):
    xb_scr[...] = x_ref[...].T.astype(BF16)
    qT_scr[...] = _dot(wqT_ref[...], xb_scr[...]).astype(BF16)
    for h in range(PEER_HEADS):
        for p, dst in ((0, s1_scr), (1, s2_scr)):
            r0 = (h * 2 + p) * (D_KEY // 2)
            dst[h] = _dot(keys_ref[p], qT_scr[r0:r0 + D_KEY // 2, :])

    def chunk_body(c, carry):
        lanes = pl.ds(pl.multiple_of(c * LANES, LANES), LANES)

        def one_head(h):
            s1 = s1_scr[h, :, lanes]
            s2 = s2_scr[h, :, lanes]
            r1, t1 = _top16_rows(s1)
            r2, t2 = _top16_rows(s2)
            tau, z = _pair_threshold(r1, t1, r2, t2)
            cnt_by_rank = jnp.zeros(t1.shape, F32)
            for jx in range(PEER_TOPK):
                cnt_by_rank = cnt_by_rank + jnp.where(t1 + r2[jx] >= tau, 1.0, 0.0)
            cnt = jnp.zeros(s1.shape, F32)
            for i in reversed(range(PEER_TOPK)):
                cnt = jnp.where(s1 >= r1[i], cnt_by_rank[i:i + 1], cnt)
            cnt_scr[h, :, lanes] = cnt
            e1_scr[h, :, lanes] = jnp.exp(s1 - r1[0]) * (1.0 / z)
            rank2_scr[h, :, lanes] = _rank_of(s2, r2).astype(BF16)
            e2_scr[h, :, lanes] = jnp.exp(s2 - r2[0]).astype(BF16)

        def head_pair_body(hp, carry2):
            one_head(2 * hp)
            one_head(2 * hp + 1)
            return carry2

        return lax.fori_loop(0, PEER_HEADS // 2, head_pair_body, carry)

    lax.fori_loop(0, PEER_TOKENS // LANES, chunk_body, 0)


def _peer_gate(j, cnt_scr, e1_scr, rank2_scr, e2_scr, w_scr):
    a_per_step = PEER_EXPERTS // N_KEYS
    a_rows = pl.ds(pl.multiple_of(j * a_per_step, a_per_step), a_per_step)
    zero = jnp.zeros((), BF16)
    for c in range(PEER_TOKENS // LANES):
        lanes = slice(c * LANES, (c + 1) * LANES)
        cnt_blk = [cnt_scr[h, a_rows, lanes] for h in range(PEER_HEADS)]
        e1_blk = [e1_scr[h, a_rows, lanes] for h in range(PEER_HEADS)]
        for al in range(a_per_step):
            w = None
            for h in range(PEER_HEADS):
                cnt = jnp.broadcast_to(cnt_blk[h][al:al + 1], (N_KEYS, LANES)).astype(BF16)
                e1 = jnp.broadcast_to(e1_blk[h][al:al + 1], (N_KEYS, LANES)).astype(BF16)
                wh = jnp.where(rank2_scr[h, :, lanes] < cnt, e2_scr[h, :, lanes] * e1, zero)
                w = wh if w is None else w + wh
            w_scr[al * N_KEYS:(al + 1) * N_KEYS, lanes] = w


def _peer_kernel(x_ref, wqT_ref, keys_ref, u_ref, vT_ref, o_ref,
                 xb_scr, qT_scr, s1_scr, s2_scr, cnt_scr, e1_scr, rank2_scr, e2_scr,
                 w_scr, hid_scr, acc_scr):
    j = pl.program_id(1)

    @pl.when(j == 0)
    def _start():
        _peer_select(x_ref, wqT_ref, keys_ref, xb_scr, qT_scr, s1_scr, s2_scr,
                     cnt_scr, e1_scr, rank2_scr, e2_scr)
        acc_scr[...] = jnp.zeros_like(acc_scr)

    hid_scr[...] = _dot(u_ref[...], xb_scr[...]).astype(BF16)
    _peer_gate(j, cnt_scr, e1_scr, rank2_scr, e2_scr, w_scr)
    total = None
    for k in range(PEER_EXPERTS // MXU_DIM):
        rows = slice(k * MXU_DIM, (k + 1) * MXU_DIM)
        g = jax.nn.gelu(hid_scr[rows, :]) * w_scr[rows, :]
        part = _dot(vT_ref[:, rows], g)
        total = part if total is None else total + part
    acc_scr[...] += total

    @pl.when(j == pl.num_programs(1) - 1)
    def _finish():
        o_ref[...] = acc_scr[...].T


def _peer_layer(h, wqT, keys, u_tab, vT_tab):
    t = h.shape[0]
    n_tiles = N_EXPERTS // PEER_EXPERTS
    return pl.pallas_call(
        _peer_kernel,
        grid=(t // PEER_TOKENS, n_tiles),
        in_specs=[
            pl.BlockSpec((PEER_TOKENS, D_MODEL), lambda i, j: (i, 0)),
            _const_spec((PEER_HEADS * D_KEY, D_MODEL)),
            _const_spec((2, N_KEYS, D_KEY // 2)),
            pl.BlockSpec((PEER_EXPERTS, D_MODEL), lambda i, j: (j, 0)),
            pl.BlockSpec((D_MODEL, PEER_EXPERTS), lambda i, j: (0, j)),
        ],
        out_specs=pl.BlockSpec((PEER_TOKENS, D_MODEL), lambda i, j: (i, 0)),
        out_shape=jax.ShapeDtypeStruct((t, D_MODEL), F32),
        scratch_shapes=[
            pltpu.VMEM((D_MODEL, PEER_TOKENS), BF16),
            pltpu.VMEM((PEER_HEADS * D_KEY, PEER_TOKENS), BF16),
            pltpu.VMEM((PEER_HEADS, N_KEYS, PEER_TOKENS), F32),
            pltpu.VMEM((PEER_HEADS, N_KEYS, PEER_TOKENS), F32),
            pltpu.VMEM((PEER_HEADS, N_KEYS, PEER_TOKENS), F32),
            pltpu.VMEM((PEER_HEADS, N_KEYS, PEER_TOKENS), F32),
            pltpu.VMEM((PEER_HEADS, N_KEYS, PEER_TOKENS), BF16),
            pltpu.VMEM((PEER_HEADS, N_KEYS, PEER_TOKENS), BF16),
            pltpu.VMEM((PEER_EXPERTS, PEER_TOKENS), BF16),
            pltpu.VMEM((PEER_EXPERTS, PEER_TOKENS), BF16),
            pltpu.VMEM((D_MODEL, PEER_TOKENS), F32),
        ],
        compiler_params=_params("arbitrary", "arbitrary"),
        name="peer_layer",
    )(h, wqT, keys, u_tab, vT_tab)


def _post_kernel(h_ref, y_ref, p_ref, g2_ref, b2_ref, wg_ref, bg_ref, wp_ref, o_ref):
    hn = _layer_norm(DEEPNORM_ALPHA * h_ref[...] + y_ref[...], g2_ref[...], b2_ref[...])
    gate = jax.nn.sigmoid(_dot(hn.astype(BF16), wg_ref[...]) + bg_ref[...])
    o_ref[...] = hn + gate * _dot(p_ref[...].astype(BF16), wp_ref[...])


def _post_layer(h, y, p_prompt, p_sample, g2, b2, wg, bg, wp):
    t = h.shape[0]
    n_prompt_tiles = p_prompt.shape[0] // POST_TOKENS
    tok = lambda i: (i, 0)
    prompt_map = lambda i: (jnp.minimum(i, n_prompt_tiles - 1), 0)
    sample_map = lambda i: (jnp.maximum(i - n_prompt_tiles, 0), 0)

    def body(h_ref, y_ref, pp_ref, ps_ref, *rest):
        is_prompt = pl.program_id(0) < n_prompt_tiles

        @pl.when(is_prompt)
        def _():
            _post_kernel(h_ref, y_ref, pp_ref, *rest)

        @pl.when(jnp.logical_not(is_prompt))
        def _():
            _post_kernel(h_ref, y_ref, ps_ref, *rest)

    return pl.pallas_call(
        body,
        grid=(t // POST_TOKENS,),
        in_specs=[
            pl.BlockSpec((POST_TOKENS, D_MODEL), tok),
            pl.BlockSpec((POST_TOKENS, D_MODEL), tok),
            pl.BlockSpec((POST_TOKENS, D_PLE), prompt_map),
            pl.BlockSpec((POST_TOKENS, D_PLE), sample_map),
            _const_spec((1, D_MODEL)),
            _const_spec((1, D_MODEL)),
            _const_spec((D_MODEL, D_MODEL)),
            _const_spec((1, D_MODEL)),
            _const_spec((D_PLE, D_MODEL)),
        ],
        out_specs=pl.BlockSpec((POST_TOKENS, D_MODEL), tok),
        out_shape=jax.ShapeDtypeStruct((t, D_MODEL), F32),
        compiler_params=_params("arbitrary"),
        name="post_layer",
    )(h, y, p_prompt, p_sample, g2, b2, wg, bg, wp)


def _row(v):
    return v.reshape(1, -1).astype(F32)


def _sgu_spatial_params(w_s, b_s, dec_seq):
    tri = jnp.tril(jnp.ones((CHUNK, CHUNK), bool))
    ws_prompt = jnp.where(tri, w_s, 0.0)
    tri_s = jnp.tril(jnp.ones((dec_seq, dec_seq), bool))
    small = jnp.where(tri_s, w_s[:, :dec_seq, :dec_seq], 0.0)
    eye = jnp.eye(CHUNK // dec_seq, dtype=w_s.dtype)
    ws_sample = jnp.einsum("ab,gts->gatbs", eye, small).reshape(SGU_GROUPS, CHUNK, CHUNK)
    ws2 = jnp.stack([ws_prompt, ws_sample]).astype(BF16)
    bs_prompt = b_s.T
    bs_sample = jnp.tile(b_s[:, :dec_seq].T, (CHUNK // dec_seq, 1))
    bs2 = jnp.stack([bs_prompt, bs_sample])
    bs2 = jnp.repeat(bs2, SGU_GROUP_DIM, axis=2).astype(F32)
    return ws2, bs2


def kernel(x_prompt, x_sample, cache_k_win, cache_v_win, p_prompt, p_sample,
           ln1_g, ln1_b, ln2_g, ln2_b,
           sgu_w_in, sgu_b_in, sgu_ln_g, sgu_ln_b, sgu_w_s, sgu_b_s, sgu_w_out, sgu_b_out,
           attn_w_kv, attn_w_q, attn_sinks, attn_w_o,
           peer_w_q, peer_subkeys, peer_u, peer_v,
           ple_w, ple_gate_w, ple_gate_b):
    batch, seq, _ = x_prompt.shape
    dec_batch, dec_seq, _ = x_sample.shape
    tp = batch * seq
    ts = dec_batch * dec_seq
    half = N_KV_HEADS * HEAD_DIM
    wc = cache_k_win.shape[1]
    assert wc == WINDOW and CHUNK % dec_seq == 0 and seq % WINDOW == 0
    assert dec_seq & (dec_seq - 1) == 0 and WINDOW & (WINDOW - 1) == 0
    assert tp % PEER_TOKENS == 0 and ts % PEER_TOKENS == 0 and ts % SGU_TOKENS == 0

    h = jnp.concatenate([x_prompt.reshape(tp, D_MODEL), x_sample.reshape(ts, D_MODEL)], axis=0)
    pp = p_prompt.reshape(DEPTH, tp, D_PLE)
    ps = p_sample.reshape(DEPTH, ts, D_PLE)
    ck = cache_k_win.reshape(dec_batch * wc, half).astype(F32)
    cv = cache_v_win.reshape(dec_batch * wc, half).astype(F32)

    sgu_rows = []
    k_all = v_all = None
    for i in range(DEPTH):
        g1, b1 = _row(ln1_g[i]), _row(ln1_b[i])
        if i < N_A_LAYERS:
            ws2, bs2 = _sgu_spatial_params(sgu_w_s[i], sgu_b_s[i], dec_seq)
            h, v_rows = _sgu_layer(
                h, tp, sgu_w_in[i].astype(BF16), _row(sgu_b_in[i]), _row(sgu_ln_g[i]),
                _row(sgu_ln_b[i]), ws2, bs2, sgu_w_out[i].astype(BF16), _row(sgu_b_out[i]), g1, b1)
            sgu_rows.append(v_rows.reshape(dec_batch, dec_seq, D_SGU))
        else:
            if i == N_A_LAYERS:
                k_all, v_all = _kv_proj(h, attn_w_kv.astype(BF16))
            jl = i - N_A_LAYERS
            wqT = attn_w_q[jl].T.astype(BF16)
            woT = attn_w_o[jl].T.astype(BF16)
            sinks = attn_sinks[jl].astype(F32)
            h = _attn_layer(h, k_all, v_all, ck, cv, sinks, wqT, woT, g1, b1,
                            n_prompt_tokens=tp, blocks_per_seq=seq // WINDOW, dec_seq=dec_seq)
        y = _peer_layer(h, peer_w_q[i].T.astype(BF16), peer_subkeys[i].astype(BF16),
                        peer_u[i].astype(BF16), peer_v[i].T.astype(BF16))
        h = _post_layer(h, y, pp[i], ps[i], _row(ln2_g[i]), _row(ln2_b[i]),
                        ple_gate_w[i].astype(BF16), _row(ple_gate_b[i]), ple_w[i].astype(BF16))

    y_prompt = h[:tp].reshape(batch, seq, D_MODEL)
    y_sample = h[tp:].reshape(dec_batch, dec_seq, D_MODEL)
    kp4 = k_all[:tp].reshape(batch, seq, N_KV_HEADS, HEAD_DIM)
    vp4 = v_all[:tp].reshape(batch, seq, N_KV_HEADS, HEAD_DIM)
    new_k_win_prompt = kp4[:, -WINDOW:]
    new_v_win_prompt = vp4[:, -WINDOW:]
    kn4 = k_all[tp:].reshape(dec_batch, dec_seq, N_KV_HEADS, HEAD_DIM)
    vn4 = v_all[tp:].reshape(dec_batch, dec_seq, N_KV_HEADS, HEAD_DIM)
    new_k_win_sample = jnp.concatenate([cache_k_win.astype(F32), kn4], axis=1)[:, -wc:]
    new_v_win_sample = jnp.concatenate([cache_v_win.astype(F32), vn4], axis=1)[:, -wc:]
    new_sgu_v_sample = jnp.stack(sgu_rows, axis=0)
    return (y_prompt, y_sample, new_k_win_prompt, new_v_win_prompt,
            new_k_win_sample, new_v_win_sample, new_sgu_v_sample)
```

```python
import functools

import jax
import jax.numpy as jnp
from jax import lax
from jax.experimental import pallas as pl
from jax.experimental.pallas import tpu as pltpu

D_MODEL = 1024
DEPTH = 4
N_A_LAYERS = DEPTH // 2
CHUNK = 128
D_SGU = 2 * D_MODEL
SGU_GROUPS = 8
SGU_GROUP_DIM = D_SGU // SGU_GROUPS
HEAD_DIM = 64
N_HEADS = D_MODEL // HEAD_DIM
N_KV_HEADS = 2
GROUP = N_HEADS // N_KV_HEADS
WINDOW = 128
PEER_HEADS = 8
N_KEYS = 128
N_EXPERTS = N_KEYS * N_KEYS
D_KEY = 256
PEER_TOPK = 16
D_PLE = 256
DEEPNORM_ALPHA = (2.0 * DEPTH) ** 0.25
LN_EPS = 1e-5

LANES = 128
MXU_DIM = 256
VMEM_LIMIT_BYTES = 56 * 1024 * 1024

SGU_TOKENS = 256
PEER_TOKENS = 512
PEER_EXPERTS = 1024
POST_TOKENS = 512
KV_TOKENS = 512

NEG_INF = float("-inf")
NT_DIMS = (((1,), (1,)), ((), ()))

BF16 = jnp.bfloat16
F32 = jnp.float32


def _layer_norm(x, g, b):
    mu = jnp.mean(x, axis=-1, keepdims=True)
    xc = x - mu
    var = jnp.mean(xc * xc, axis=-1, keepdims=True)
    return xc * lax.rsqrt(var + LN_EPS) * g + b


def _dot(a, b):
    return jnp.dot(a, b, preferred_element_type=F32)


def _dot_nt(a, b):
    return lax.dot_general(a, b, NT_DIMS, preferred_element_type=F32)


def _const_spec(shape):
    n = len(shape)
    return pl.BlockSpec(shape, lambda *_: (0,) * n)


def _params(*semantics):
    return pltpu.CompilerParams(dimension_semantics=semantics,
                                vmem_limit_bytes=VMEM_LIMIT_BYTES)


def _sgu_kernel(x_ref, win_ref, bin_ref, lng_ref, lnb_ref, ws_ref, bs_ref,
                wout_ref, bout_ref, g1_ref, b1_ref, h_ref, v_ref,
                u_scr, v_scr):
    x = x_ref[...]
    xb = x.astype(BF16)
    n_col = (2 * D_SGU) // 512
    for j in range(n_col):
        z = _dot(xb, win_ref[:, j * 512:(j + 1) * 512]) + bin_ref[:, j * 512:(j + 1) * 512]
        z = jax.nn.gelu(z)
        if j < n_col // 2:
            u_scr[:, j * 512:(j + 1) * 512] = z
        else:
            jj = j - n_col // 2
            v_scr[:, jj * 512:(jj + 1) * 512] = z
    v = _layer_norm(v_scr[...], lng_ref[...], lnb_ref[...])
    v_ref[...] = v
    v_scr[...] = v
    for c in range(SGU_TOKENS // CHUNK):
        rows = slice(c * CHUNK, (c + 1) * CHUNK)
        for g in range(SGU_GROUPS):
            cols = slice(g * SGU_GROUP_DIM, (g + 1) * SGU_GROUP_DIM)
            mixed = _dot(ws_ref[0, g], v_scr[rows, cols].astype(BF16)) + bs_ref[0, :, cols]
            u_scr[rows, cols] = u_scr[rows, cols] * mixed
    out = _dot(u_scr[...].astype(BF16), wout_ref[...]) + bout_ref[...]
    h_ref[...] = _layer_norm(DEEPNORM_ALPHA * x + out, g1_ref[...], b1_ref[...])


def _sgu_layer(h, n_prompt_tokens, win, b_in, ln_g, ln_b, ws2, bs2, wout, b_out, g1, b1):
    t = h.shape[0]
    n_tiles = t // SGU_TOKENS
    n_prompt_tiles = n_prompt_tokens // SGU_TOKENS
    t_sample = t - n_prompt_tokens

    def kind(i):
        return jnp.where(i < n_prompt_tiles, 0, 1)

    return pl.pallas_call(
        _sgu_kernel,
        grid=(n_tiles,),
        in_specs=[
            pl.BlockSpec((SGU_TOKENS, D_MODEL), lambda i: (i, 0)),
            _const_spec((D_MODEL, 2 * D_SGU)),
            _const_spec((1, 2 * D_SGU)),
            _const_spec((1, D_SGU)),
            _const_spec((1, D_SGU)),
            pl.BlockSpec((1, SGU_GROUPS, CHUNK, CHUNK), lambda i: (kind(i), 0, 0, 0)),
            pl.BlockSpec((1, CHUNK, D_SGU), lambda i: (kind(i), 0, 0)),
            _const_spec((D_SGU, D_MODEL)),
            _const_spec((1, D_MODEL)),
            _const_spec((1, D_MODEL)),
            _const_spec((1, D_MODEL)),
        ],
        out_specs=[
            pl.BlockSpec((SGU_TOKENS, D_MODEL), lambda i: (i, 0)),
            pl.BlockSpec((SGU_TOKENS, D_SGU), lambda i: (jnp.maximum(i - n_prompt_tiles, 0), 0)),
        ],
        out_shape=[
            jax.ShapeDtypeStruct((t, D_MODEL), F32),
            jax.ShapeDtypeStruct((t_sample, D_SGU), F32),
        ],
        scratch_shapes=[
            pltpu.VMEM((SGU_TOKENS, D_SGU), F32),
            pltpu.VMEM((SGU_TOKENS, D_SGU), F32),
        ],
        compiler_params=_params("arbitrary"),
        name="sgu_layer",
    )(h, win, b_in, ln_g, ln_b, ws2, bs2, wout, b_out, g1, b1)


def _kv_kernel(h_ref, w_ref, k_ref, v_ref):
    kv = _dot(h_ref[...].astype(BF16), w_ref[...])
    half = N_KV_HEADS * HEAD_DIM
    k_ref[...] = kv[:, :half]
    v_ref[...] = kv[:, half:]


def _kv_proj(h, w_kv):
    t = h.shape[0]
    half = N_KV_HEADS * HEAD_DIM
    return pl.pallas_call(
        _kv_kernel,
        grid=(t // KV_TOKENS,),
        in_specs=[
            pl.BlockSpec((KV_TOKENS, D_MODEL), lambda i: (i, 0)),
            _const_spec((D_MODEL, 2 * half)),
        ],
        out_specs=[
            pl.BlockSpec((KV_TOKENS, half), lambda i: (i, 0)),
            pl.BlockSpec((KV_TOKENS, half), lambda i: (i, 0)),
        ],
        out_shape=[jax.ShapeDtypeStruct((t, half), F32)] * 2,
        compiler_params=_params("arbitrary"),
        name="kv_proj",
    )(h, w_kv)


def _sink_softmax_rows(s, valid, sink):
    s = jnp.where(valid, s, NEG_INF)
    m = jnp.maximum(jnp.max(s, axis=0, keepdims=True), sink)
    p = jnp.exp(s - m)
    den = jnp.sum(p, axis=0, keepdims=True) + jnp.exp(sink - m)
    return p / den


def _alibi_slope(h):
    return 2.0 ** (-8.0 * (h + 1) / N_HEADS)


def _attn_prompt_kernel(sink_ref, h_ref, kp_ref, kc_ref, vp_ref, vc_ref, wqT_ref, woT_ref,
                        g1_ref, b1_ref, o_ref, oT_scr, *, blocks_per_seq):
    hb = h_ref[...]
    qT = _dot_nt(wqT_ref[...], hb.astype(BF16)).astype(BF16)
    kk = jnp.concatenate([kp_ref[...], kc_ref[...]], axis=0).astype(BF16)
    vv = jnp.concatenate([vp_ref[...], vc_ref[...]], axis=0)
    vvT = vv.T.astype(BF16)

    kj = lax.broadcasted_iota(jnp.int32, (2 * WINDOW, WINDOW), 0)
    qi = lax.broadcasted_iota(jnp.int32, (2 * WINDOW, WINDOW), 1)
    dist = qi + WINDOW - kj
    is_first = (pl.program_id(0) % blocks_per_seq == 0).astype(jnp.int32)
    valid = (dist >= 0) & (dist < WINDOW) & (kj >= is_first * WINDOW)
    distf = dist.astype(F32)

    zeros_q = jnp.zeros((HEAD_DIM, GROUP * WINDOW), BF16)
    for k in range(N_KV_HEADS):
        heads = range(k * GROUP, (k + 1) * GROUP)
        q_grp = jnp.concatenate([qT[h * HEAD_DIM:(h + 1) * HEAD_DIM] for h in heads], axis=1)
        q_pad = jnp.concatenate([q_grp, zeros_q] if k == 0 else [zeros_q, q_grp], axis=0)
        s_all = _dot(kk, q_pad) * (HEAD_DIM ** -0.5)
        p_blocks = []
        for g, h in enumerate(heads):
            s = s_all[:, g * WINDOW:(g + 1) * WINDOW] - _alibi_slope(h) * distf
            p_blocks.append(_sink_softmax_rows(s, valid, sink_ref[h]).astype(BF16))
        o_all = _dot(vvT[k * HEAD_DIM:(k + 1) * HEAD_DIM],
                     jnp.concatenate(p_blocks, axis=1))
        for g, h in enumerate(heads):
            oT_scr[h * HEAD_DIM:(h + 1) * HEAD_DIM, :] = o_all[:, g * WINDOW:(g + 1) * WINDOW]
    outT = _dot(woT_ref[...], oT_scr[...].astype(BF16))
    o_ref[...] = _layer_norm(DEEPNORM_ALPHA * hb + outT.T, g1_ref[...], b1_ref[...])


def _attn_sample_kernel(sink_ref, h_ref, ck_ref, kn_ref, cv_ref, vn_ref, wqT_ref, woT_ref,
                        g1_ref, b1_ref, o_ref, oT_scr, *, dec_seq):
    n_seq = WINDOW // dec_seq
    n_cache = n_seq * WINDOW
    n_keys = n_cache + WINDOW
    hb = h_ref[...]
    qT = _dot_nt(wqT_ref[...], hb.astype(BF16)).astype(BF16)
    kk = jnp.concatenate([ck_ref[...], kn_ref[...]], axis=0).astype(BF16)
    vv = jnp.concatenate([cv_ref[...], vn_ref[...]], axis=0)
    vvT = vv.T.astype(BF16)

    kidx = lax.broadcasted_iota(jnp.int32, (n_keys, WINDOW), 0)
    q = lax.broadcasted_iota(jnp.int32, (n_keys, WINDOW), 1)
    seq_shift, win_shift = dec_seq.bit_length() - 1, WINDOW.bit_length() - 1
    q_seq, q_pos = q >> seq_shift, q & (dec_seq - 1)
    is_cache = kidx < n_cache
    new_idx = jnp.maximum(kidx - n_cache, 0)
    k_seq = jnp.where(is_cache, kidx >> win_shift, new_idx >> seq_shift)
    dist = jnp.where(is_cache, q_pos + WINDOW - (kidx & (WINDOW - 1)),
                     q_pos - (new_idx & (dec_seq - 1)))
    valid = (k_seq == q_seq) & (dist >= 0) & (dist < WINDOW)
    distf = dist.astype(F32)

    zeros_q = jnp.zeros((HEAD_DIM, WINDOW), BF16)
    for h in range(N_HEADS):
        k = h // GROUP
        q_h = qT[h * HEAD_DIM:(h + 1) * HEAD_DIM]
        q_pad = jnp.concatenate([q_h, zeros_q] if k == 0 else [zeros_q, q_h], axis=0)
        s = _dot(kk, q_pad) * (HEAD_DIM ** -0.5) - _alibi_slope(h) * distf
        p = _sink_softmax_rows(s, valid, sink_ref[h]).astype(BF16)
        oT_scr[h * HEAD_DIM:(h + 1) * HEAD_DIM, :] = _dot(vvT[k * HEAD_DIM:(k + 1) * HEAD_DIM], p)
    outT = _dot(woT_ref[...], oT_scr[...].astype(BF16))
    o_ref[...] = _layer_norm(DEEPNORM_ALPHA * hb + outT.T, g1_ref[...], b1_ref[...])


def _attn_layer(h, k_all, v_all, cache_k, cache_v, sinks, wqT, woT, g1, b1,
                *, n_prompt_tokens, blocks_per_seq, dec_seq):
    t = h.shape[0]
    half = N_KV_HEADS * HEAD_DIM
    n_prompt_blocks = n_prompt_tokens // WINDOW
    n_sample_blocks = (t - n_prompt_tokens) // WINDOW
    n_seq = WINDOW // dec_seq
    cur_map = lambda i: (i, 0)
    prev_map = lambda i: (jnp.maximum(i - 1, 0), 0)
    cache_map = lambda i: (jnp.clip(i - n_prompt_blocks, 0, n_sample_blocks - 1), 0)

    def body(sink_ref, h_ref, kp_ref, kc_ref, vp_ref, vc_ref, ck_ref, cv_ref, *rest):
        is_prompt = pl.program_id(0) < n_prompt_blocks

        @pl.when(is_prompt)
        def _():
            _attn_prompt_kernel(sink_ref, h_ref, kp_ref, kc_ref, vp_ref, vc_ref, *rest,
                                blocks_per_seq=blocks_per_seq)

        @pl.when(jnp.logical_not(is_prompt))
        def _():
            _attn_sample_kernel(sink_ref, h_ref, ck_ref, kc_ref, cv_ref, vc_ref, *rest,
                                dec_seq=dec_seq)

    return pl.pallas_call(
        body,
        grid=(n_prompt_blocks + n_sample_blocks,),
        in_specs=[
            pl.BlockSpec(memory_space=pltpu.SMEM),
            pl.BlockSpec((WINDOW, D_MODEL), cur_map),
            pl.BlockSpec((WINDOW, half), prev_map),
            pl.BlockSpec((WINDOW, half), cur_map),
            pl.BlockSpec((WINDOW, half), prev_map),
            pl.BlockSpec((WINDOW, half), cur_map),
            pl.BlockSpec((n_seq * WINDOW, half), cache_map),
            pl.BlockSpec((n_seq * WINDOW, half), cache_map),
            _const_spec((D_MODEL, D_MODEL)),
            _const_spec((D_MODEL, D_MODEL)),
            _const_spec((1, D_MODEL)),
            _const_spec((1, D_MODEL)),
        ],
        out_specs=pl.BlockSpec((WINDOW, D_MODEL), cur_map),
        out_shape=jax.ShapeDtypeStruct((t, D_MODEL), F32),
        scratch_shapes=[pltpu.VMEM((D_MODEL, WINDOW), F32)],
        compiler_params=_params("arbitrary"),
        name="attn_layer",
    )(sinks, h, k_all, k_all, v_all, v_all, cache_k, cache_v, wqT, woT, g1, b1)


NOT_RANKED = 127.0


def _top16_rows(s):
    n = N_KEYS // 8
    v = [s[8 * i:8 * (i + 1)] for i in range(n)]
    k = 2
    while k <= n:
        j = k // 2
        while j >= 1:
            for i in range(n):
                l = i ^ j
                if l > i:
                    hi, lo = jnp.maximum(v[i], v[l]), jnp.minimum(v[i], v[l])
                    v[i], v[l] = (hi, lo) if (i & k) == 0 else (lo, hi)
            j //= 2
        k *= 2
    row_id = lax.broadcasted_iota(jnp.int32, (PEER_TOPK, LANES), 0)
    rows = []
    stacked = jnp.full((PEER_TOPK, LANES), NEG_INF, F32)
    for i in range(PEER_TOPK):
        m = jnp.max(v[0], axis=0, keepdims=True)
        rows.append(m)
        stacked = jnp.where(row_id == i, m, stacked)
        hit = v[0] == m
        for d in range(PEER_TOPK - 1 - i):
            v[d] = jnp.where(hit, v[d + 1], v[d])
    return rows, stacked


def _rank_of(s, rows):
    rank = jnp.full(s.shape, NOT_RANKED, F32)
    for i in reversed(range(len(rows))):
        rank = jnp.where(s >= rows[i], float(i), rank)
    return rank


def _pair_threshold(r1, t1, r2, t2):
    row8 = lax.broadcasted_iota(jnp.int32, (8, LANES), 0)
    cands = []
    for i in range(8):
        n = PEER_TOPK // (i + 1)
        c = r1[i] + t2[0:8]
        if n < 8:
            c = jnp.where(row8 < n, c, NEG_INF)
        cands.append(c)
    cands.append(r1[0] + t2[8:16])
    cands.append(t1[8:16] + r2[0])
    top = r1[0] + r2[0]
    z = jnp.zeros((1, LANES), F32)
    m = top
    for _ in range(PEER_TOPK):
        m = cands[0]
        for c in cands[1:]:
            m = jnp.maximum(m, c)
        m = jnp.max(m, axis=0, keepdims=True)
        z = z + jnp.exp(m - top)
        cands = [jnp.where(c == m, NEG_INF, c) for c in cands]
    return m, z


def _bf16_pair_words(v):
    hi = pltpu.bitcast(v.astype(BF16).astype(F32), jnp.uint32)
    return hi | (hi >> 16)


def _peer_select(x_ref, wqT_ref, keys_ref, xb_scr, qT_scr, s1_scr, s2_scr,
                 cnt_scr, e1_scr, rank2_scr, e2_scr):
    xb_scr[...] = x_ref[...].T.astype(BF16)
    qT_scr[...] = _dot(wqT_ref[...], xb_scr[...]).astype(BF16)
    for h in range(PEER_HEADS):
        for p, dst in ((0, s1_scr), (1, s2_scr)):
            r0 = (h * 2 + p) * (D_KEY // 2)
            dst[h] = _dot(keys_ref[p], qT_scr[r0:r0 + D_KEY // 2, :])

    def chunk_body(c, carry):
        lanes = pl.ds(pl.multiple_of(c * LANES, LANES), LANES)

        def one_head(h):
            s1 = s1_scr[h, :, lanes]
            s2 = s2_scr[h, :, lanes]
            r1, t1 = _top16_rows(s1)
            r2, t2 = _top16_rows(s2)
            tau, z = _pair_threshold(r1, t1, r2, t2)
            cnt_by_rank = jnp.zeros(t1.shape, F32)
            for jx in range(PEER_TOPK):
                cnt_by_rank = cnt_by_rank + jnp.where(t1 + r2[jx] >= tau, 1.0, 0.0)
            cnt = jnp.zeros(s1.shape, F32)
            for i in reversed(range(PEER_TOPK)):
                cnt = jnp.where(s1 >= r1[i], cnt_by_rank[i:i + 1], cnt)
            cnt_scr[h, :, lanes] = _bf16_pair_words(cnt)
            e1_scr[h, :, lanes] = _bf16_pair_words(jnp.exp(s1 - r1[0]) * (0.5 / z))
            rank2_scr[h, :, lanes] = _rank_of(s2, r2).astype(BF16)
            e2_scr[h, :, lanes] = jnp.exp(s2 - r2[0]).astype(BF16)

        def head_pair_body(hp, carry2):
            one_head(2 * hp)
            one_head(2 * hp + 1)
            return carry2

        return lax.fori_loop(0, PEER_HEADS // 2, head_pair_body, carry)

    lax.fori_loop(0, PEER_TOKENS // LANES, chunk_body, 0)


def _bf16_rows_from_pair_words(row):
    tile = pltpu.bitcast(jnp.broadcast_to(row, (8, LANES)), BF16)
    return jnp.concatenate([tile] * (N_KEYS // 16), axis=0)


def _peer_gate(j, cnt_scr, e1_scr, rank2_scr, e2_scr, w_scr):
    a_per_step = PEER_EXPERTS // N_KEYS
    a_rows = pl.ds(pl.multiple_of(j * a_per_step, a_per_step), a_per_step)
    zero = jnp.zeros((), BF16)
    for c in range(PEER_TOKENS // LANES):
        lanes = slice(c * LANES, (c + 1) * LANES)
        cnt_blk = [cnt_scr[h, a_rows, lanes] for h in range(PEER_HEADS)]
        e1_blk = [e1_scr[h, a_rows, lanes] for h in range(PEER_HEADS)]
        for al in range(a_per_step):
            w = None
            for h in range(PEER_HEADS):
                cnt = _bf16_rows_from_pair_words(cnt_blk[h][al:al + 1])
                e1 = _bf16_rows_from_pair_words(e1_blk[h][al:al + 1])
                wh = jnp.where(rank2_scr[h, :, lanes] < cnt, e2_scr[h, :, lanes] * e1, zero)
                w = wh if w is None else w + wh
            w_scr[al * N_KEYS:(al + 1) * N_KEYS, lanes] = w


GELU_C0 = 0.7978845608028654
GELU_C1 = GELU_C0 * 0.044715


def _gelu_times_half_weight(x, w_half):
    th = jnp.tanh(x * (x * x * GELU_C1 + GELU_C0))
    xw = x * w_half
    return xw * th + xw


def _peer_kernel(x_ref, wqT_ref, keys_ref, u_ref, vT_ref, o_ref,
                 xb_scr, qT_scr, s1_scr, s2_scr, cnt_scr, e1_scr, rank2_scr, e2_scr,
                 w_scr, hid_scr, acc_scr):
    j = pl.program_id(1)

    @pl.when(j == 0)
    def _start():
        _peer_select(x_ref, wqT_ref, keys_ref, xb_scr, qT_scr, s1_scr, s2_scr,
                     cnt_scr, e1_scr, rank2_scr, e2_scr)
        acc_scr[...] = jnp.zeros_like(acc_scr)

    hid_scr[...] = _dot(u_ref[0], xb_scr[...]).astype(BF16)
    _peer_gate(j, cnt_scr, e1_scr, rank2_scr, e2_scr, w_scr)
    total = None
    for k in range(PEER_EXPERTS // MXU_DIM):
        rows = slice(k * MXU_DIM, (k + 1) * MXU_DIM)
        g = _gelu_times_half_weight(hid_scr[rows, :], w_scr[rows, :])
        part = _dot(vT_ref[0, :, rows], g)
        total = part if total is None else total + part
    acc_scr[...] += total

    @pl.when(j == pl.num_programs(1) - 1)
    def _finish():
        o_ref[...] = acc_scr[...].T


def _peer_layer(h, wqT, keys, u_tabs, vT_tabs, layer):
    t = h.shape[0]
    n_tiles = N_EXPERTS // PEER_EXPERTS
    return pl.pallas_call(
        _peer_kernel,
        grid=(t // PEER_TOKENS, n_tiles),
        in_specs=[
            pl.BlockSpec((PEER_TOKENS, D_MODEL), lambda i, j: (i, 0)),
            _const_spec((PEER_HEADS * D_KEY, D_MODEL)),
            _const_spec((2, N_KEYS, D_KEY // 2)),
            pl.BlockSpec((1, PEER_EXPERTS, D_MODEL), lambda i, j: (layer, j, 0)),
            pl.BlockSpec((1, D_MODEL, PEER_EXPERTS), lambda i, j: (layer, 0, j)),
        ],
        out_specs=pl.BlockSpec((PEER_TOKENS, D_MODEL), lambda i, j: (i, 0)),
        out_shape=jax.ShapeDtypeStruct((t, D_MODEL), F32),
        scratch_shapes=[
            pltpu.VMEM((D_MODEL, PEER_TOKENS), BF16),
            pltpu.VMEM((PEER_HEADS * D_KEY, PEER_TOKENS), BF16),
            pltpu.VMEM((PEER_HEADS, N_KEYS, PEER_TOKENS), F32),
            pltpu.VMEM((PEER_HEADS, N_KEYS, PEER_TOKENS), F32),
            pltpu.VMEM((PEER_HEADS, N_KEYS, PEER_TOKENS), jnp.uint32),
            pltpu.VMEM((PEER_HEADS, N_KEYS, PEER_TOKENS), jnp.uint32),
            pltpu.VMEM((PEER_HEADS, N_KEYS, PEER_TOKENS), BF16),
            pltpu.VMEM((PEER_HEADS, N_KEYS, PEER_TOKENS), BF16),
            pltpu.VMEM((PEER_EXPERTS, PEER_TOKENS), BF16),
            pltpu.VMEM((PEER_EXPERTS, PEER_TOKENS), BF16),
            pltpu.VMEM((D_MODEL, PEER_TOKENS), F32),
        ],
        compiler_params=_params("arbitrary", "arbitrary"),
        name="peer_layer",
    )(h, wqT, keys, u_tabs, vT_tabs)


def _post_kernel(h_ref, y_ref, p_ref, g2_ref, b2_ref, wg_ref, bg_ref, wp_ref, o_ref):
    hn = _layer_norm(DEEPNORM_ALPHA * h_ref[...] + y_ref[...], g2_ref[...], b2_ref[...])
    gate = jax.nn.sigmoid(_dot(hn.astype(BF16), wg_ref[...]) + bg_ref[...])
    o_ref[...] = hn + gate * _dot(p_ref[0].astype(BF16), wp_ref[...])


def _post_layer(h, y, p_prompt, p_sample, layer, g2, b2, wg, bg, wp, *, split_output):
    t = h.shape[0]
    tp = p_prompt.shape[1]
    n_prompt_tiles = tp // POST_TOKENS
    tok = lambda i: (i, 0)
    prompt_tile = lambda i: jnp.minimum(i, n_prompt_tiles - 1)
    sample_tile = lambda i: jnp.maximum(i - n_prompt_tiles, 0)

    def body(h_ref, y_ref, pp_ref, ps_ref, g2_ref, b2_ref, wg_ref, bg_ref, wp_ref, *outs):
        is_prompt = pl.program_id(0) < n_prompt_tiles
        weights = (g2_ref, b2_ref, wg_ref, bg_ref, wp_ref)

        @pl.when(is_prompt)
        def _():
            _post_kernel(h_ref, y_ref, pp_ref, *weights, outs[0])

        @pl.when(jnp.logical_not(is_prompt))
        def _():
            _post_kernel(h_ref, y_ref, ps_ref, *weights, outs[-1])

    if split_output:
        out_specs = [pl.BlockSpec((POST_TOKENS, D_MODEL), lambda i: (prompt_tile(i), 0)),
                     pl.BlockSpec((POST_TOKENS, D_MODEL), lambda i: (sample_tile(i), 0))]
        out_shape = [jax.ShapeDtypeStruct((tp, D_MODEL), F32),
                     jax.ShapeDtypeStruct((t - tp, D_MODEL), F32)]
    else:
        out_specs = pl.BlockSpec((POST_TOKENS, D_MODEL), tok)
        out_shape = jax.ShapeDtypeStruct((t, D_MODEL), F32)
    return pl.pallas_call(
        body,
        grid=(t // POST_TOKENS,),
        in_specs=[
            pl.BlockSpec((POST_TOKENS, D_MODEL), tok),
            pl.BlockSpec((POST_TOKENS, D_MODEL), tok),
            pl.BlockSpec((1, POST_TOKENS, D_PLE), lambda i: (layer, prompt_tile(i), 0)),
            pl.BlockSpec((1, POST_TOKENS, D_PLE), lambda i: (layer, sample_tile(i), 0)),
            _const_spec((1, D_MODEL)),
            _const_spec((1, D_MODEL)),
            _const_spec((D_MODEL, D_MODEL)),
            _const_spec((1, D_MODEL)),
            _const_spec((D_PLE, D_MODEL)),
        ],
        out_specs=out_specs,
        out_shape=out_shape,
        compiler_params=_params("arbitrary"),
        name="post_layer",
    )(h, y, p_prompt, p_sample, g2, b2, wg, bg, wp)


def _row(v):
    return v.reshape(1, -1).astype(F32)


def _sgu_spatial_params(w_s, b_s, dec_seq):
    tri = jnp.tril(jnp.ones((CHUNK, CHUNK), bool))
    ws_prompt = jnp.where(tri, w_s, 0.0)
    tri_s = jnp.tril(jnp.ones((dec_seq, dec_seq), bool))
    small = jnp.where(tri_s, w_s[:, :dec_seq, :dec_seq], 0.0)
    eye = jnp.eye(CHUNK // dec_seq, dtype=w_s.dtype)
    ws_sample = jnp.einsum("ab,gts->gatbs", eye, small).reshape(SGU_GROUPS, CHUNK, CHUNK)
    ws2 = jnp.stack([ws_prompt, ws_sample]).astype(BF16)
    bs_prompt = b_s.T
    bs_sample = jnp.tile(b_s[:, :dec_seq].T, (CHUNK // dec_seq, 1))
    bs2 = jnp.stack([bs_prompt, bs_sample])
    bs2 = jnp.repeat(bs2, SGU_GROUP_DIM, axis=2).astype(F32)
    return ws2, bs2


def kernel(x_prompt, x_sample, cache_k_win, cache_v_win, p_prompt, p_sample,
           ln1_g, ln1_b, ln2_g, ln2_b,
           sgu_w_in, sgu_b_in, sgu_ln_g, sgu_ln_b, sgu_w_s, sgu_b_s, sgu_w_out, sgu_b_out,
           attn_w_kv, attn_w_q, attn_sinks, attn_w_o,
           peer_w_q, peer_subkeys, peer_u, peer_v,
           ple_w, ple_gate_w, ple_gate_b):
    batch, seq, _ = x_prompt.shape
    dec_batch, dec_seq, _ = x_sample.shape
    tp = batch * seq
    ts = dec_batch * dec_seq
    half = N_KV_HEADS * HEAD_DIM
    wc = cache_k_win.shape[1]
    assert wc == WINDOW and CHUNK % dec_seq == 0 and seq % WINDOW == 0
    assert dec_seq & (dec_seq - 1) == 0 and WINDOW & (WINDOW - 1) == 0
    assert tp % PEER_TOKENS == 0 and ts % PEER_TOKENS == 0 and ts % SGU_TOKENS == 0

    h = jnp.concatenate([x_prompt.reshape(tp, D_MODEL), x_sample.reshape(ts, D_MODEL)], axis=0)
    pp = p_prompt.reshape(DEPTH, tp, D_PLE)
    ps = p_sample.reshape(DEPTH, ts, D_PLE)
    ck = cache_k_win.reshape(dec_batch * wc, half).astype(F32)
    cv = cache_v_win.reshape(dec_batch * wc, half).astype(F32)
    u_tabs = peer_u.astype(BF16)
    vT_tabs = jnp.swapaxes(peer_v, 1, 2).astype(BF16)

    sgu_rows = []
    k_all = v_all = None
    for i in range(DEPTH):
        g1, b1 = _row(ln1_g[i]), _row(ln1_b[i])
        if i < N_A_LAYERS:
            ws2, bs2 = _sgu_spatial_params(sgu_w_s[i], sgu_b_s[i], dec_seq)
            h, v_rows = _sgu_layer(
                h, tp, sgu_w_in[i].astype(BF16), _row(sgu_b_in[i]), _row(sgu_ln_g[i]),
                _row(sgu_ln_b[i]), ws2, bs2, sgu_w_out[i].astype(BF16), _row(sgu_b_out[i]), g1, b1)
            sgu_rows.append(v_rows.reshape(dec_batch, dec_seq, D_SGU))
        else:
            if i == N_A_LAYERS:
                k_all, v_all = _kv_proj(h, attn_w_kv.astype(BF16))
            jl = i - N_A_LAYERS
            wqT = attn_w_q[jl].T.astype(BF16)
            woT = attn_w_o[jl].T.astype(BF16)
            sinks = attn_sinks[jl].astype(F32)
            h = _attn_layer(h, k_all, v_all, ck, cv, sinks, wqT, woT, g1, b1,
                            n_prompt_tokens=tp, blocks_per_seq=seq // WINDOW, dec_seq=dec_seq)
        y = _peer_layer(h, peer_w_q[i].T.astype(BF16), peer_subkeys[i].astype(BF16),
                        u_tabs, vT_tabs, i)
        h = _post_layer(h, y, pp, ps, i, _row(ln2_g[i]), _row(ln2_b[i]),
                        ple_gate_w[i].astype(BF16), _row(ple_gate_b[i]), ple_w[i].astype(BF16),
                        split_output=(i == DEPTH - 1))

    y_prompt = h[0].reshape(batch, seq, D_MODEL)
    y_sample = h[1].reshape(dec_batch, dec_seq, D_MODEL)
    kp4 = k_all[:tp].reshape(batch, seq, N_KV_HEADS, HEAD_DIM)
    vp4 = v_all[:tp].reshape(batch, seq, N_KV_HEADS, HEAD_DIM)
    new_k_win_prompt = kp4[:, -WINDOW:]
    new_v_win_prompt = vp4[:, -WINDOW:]
    kn4 = k_all[tp:].reshape(dec_batch, dec_seq, N_KV_HEADS, HEAD_DIM)
    vn4 = v_all[tp:].reshape(dec_batch, dec_seq, N_KV_HEADS, HEAD_DIM)
    new_k_win_sample = jnp.concatenate([cache_k_win.astype(F32), kn4], axis=1)[:, -wc:]
    new_v_win_sample = jnp.concatenate([cache_v_win.astype(F32), vn4], axis=1)[:, -wc:]
    new_sgu_v_sample = jnp.stack(sgu_rows, axis=0)
    return (y_prompt, y_sample, new_k_win_prompt, new_v_win_prompt,
            new_k_win_sample, new_v_win_sample, new_sgu_v_sample)
```

```python
import functools

import jax
import jax.numpy as jnp
from jax import lax
from jax.experimental import pallas as pl
from jax.experimental.pallas import tpu as pltpu

D_MODEL = 1024
DEPTH = 4
N_A_LAYERS = DEPTH // 2
CHUNK = 128
D_SGU = 2 * D_MODEL
SGU_GROUPS = 8
SGU_GROUP_DIM = D_SGU // SGU_GROUPS
HEAD_DIM = 64
N_HEADS = D_MODEL // HEAD_DIM
N_KV_HEADS = 2
GROUP = N_HEADS // N_KV_HEADS
WINDOW = 128
PEER_HEADS = 8
N_KEYS = 128
N_EXPERTS = N_KEYS * N_KEYS
D_KEY = 256
PEER_TOPK = 16
D_PLE = 256
DEEPNORM_ALPHA = (2.0 * DEPTH) ** 0.25
LN_EPS = 1e-5

LANES = 128
MXU_DIM = 256
VMEM_LIMIT_BYTES = 56 * 1024 * 1024

SGU_TOKENS = 256
PEER_TOKENS = 512
PEER_EXPERTS = 1024
POST_TOKENS = 512
KV_TOKENS = 512
ATTN_BLOCKS = 2

NEG_INF = float("-inf")
NT_DIMS = (((1,), (1,)), ((), ()))

BF16 = jnp.bfloat16
F32 = jnp.float32


def _layer_norm(x, g, b):
    mu = jnp.mean(x, axis=-1, keepdims=True)
    xc = x - mu
    var = jnp.mean(xc * xc, axis=-1, keepdims=True)
    return xc * lax.rsqrt(var + LN_EPS) * g + b


def _dot(a, b):
    return jnp.dot(a, b, preferred_element_type=F32)


def _dot_nt(a, b):
    return lax.dot_general(a, b, NT_DIMS, preferred_element_type=F32)


def _const_spec(shape):
    n = len(shape)
    return pl.BlockSpec(shape, lambda *_: (0,) * n)


def _params(*semantics):
    return pltpu.CompilerParams(dimension_semantics=semantics,
                                vmem_limit_bytes=VMEM_LIMIT_BYTES)


def _sgu_kernel(x_ref, win_ref, bin_ref, lng_ref, lnb_ref, ws_ref, bs_ref,
                wout_ref, bout_ref, g1_ref, b1_ref, h_ref, v_ref,
                u_scr, v_scr):
    x = x_ref[...]
    xb = x.astype(BF16)
    n_col = (2 * D_SGU) // 512
    for j in range(n_col):
        z = _dot(xb, win_ref[:, j * 512:(j + 1) * 512]) + bin_ref[:, j * 512:(j + 1) * 512]
        z = jax.nn.gelu(z)
        if j < n_col // 2:
            u_scr[:, j * 512:(j + 1) * 512] = z
        else:
            jj = j - n_col // 2
            v_scr[:, jj * 512:(jj + 1) * 512] = z
    v = _layer_norm(v_scr[...], lng_ref[...], lnb_ref[...])
    v_ref[...] = v
    v_scr[...] = v
    for c in range(SGU_TOKENS // CHUNK):
        rows = slice(c * CHUNK, (c + 1) * CHUNK)
        for g in range(SGU_GROUPS):
            cols = slice(g * SGU_GROUP_DIM, (g + 1) * SGU_GROUP_DIM)
            mixed = _dot(ws_ref[0, g], v_scr[rows, cols].astype(BF16)) + bs_ref[0, :, cols]
            u_scr[rows, cols] = u_scr[rows, cols] * mixed
    out = _dot(u_scr[...].astype(BF16), wout_ref[...]) + bout_ref[...]
    h_ref[...] = _layer_norm(DEEPNORM_ALPHA * x + out, g1_ref[...], b1_ref[...])


def _sgu_layer(h, n_prompt_tokens, win, b_in, ln_g, ln_b, ws2, bs2, wout, b_out, g1, b1):
    t = h.shape[0]
    n_tiles = t // SGU_TOKENS
    n_prompt_tiles = n_prompt_tokens // SGU_TOKENS
    t_sample = t - n_prompt_tokens

    def kind(i):
        return jnp.where(i < n_prompt_tiles, 0, 1)

    return pl.pallas_call(
        _sgu_kernel,
        grid=(n_tiles,),
        in_specs=[
            pl.BlockSpec((SGU_TOKENS, D_MODEL), lambda i: (i, 0)),
            _const_spec((D_MODEL, 2 * D_SGU)),
            _const_spec((1, 2 * D_SGU)),
            _const_spec((1, D_SGU)),
            _const_spec((1, D_SGU)),
            pl.BlockSpec((1, SGU_GROUPS, CHUNK, CHUNK), lambda i: (kind(i), 0, 0, 0)),
            pl.BlockSpec((1, CHUNK, D_SGU), lambda i: (kind(i), 0, 0)),
            _const_spec((D_SGU, D_MODEL)),
            _const_spec((1, D_MODEL)),
            _const_spec((1, D_MODEL)),
            _const_spec((1, D_MODEL)),
        ],
        out_specs=[
            pl.BlockSpec((SGU_TOKENS, D_MODEL), lambda i: (i, 0)),
            pl.BlockSpec((SGU_TOKENS, D_SGU), lambda i: (jnp.maximum(i - n_prompt_tiles, 0), 0)),
        ],
        out_shape=[
            jax.ShapeDtypeStruct((t, D_MODEL), F32),
            jax.ShapeDtypeStruct((t_sample, D_SGU), F32),
        ],
        scratch_shapes=[
            pltpu.VMEM((SGU_TOKENS, D_SGU), F32),
            pltpu.VMEM((SGU_TOKENS, D_SGU), F32),
        ],
        compiler_params=_params("arbitrary"),
        name="sgu_layer",
    )(h, win, b_in, ln_g, ln_b, ws2, bs2, wout, b_out, g1, b1)


def _kv_kernel(h_ref, w_ref, k_ref, v_ref):
    kv = _dot(h_ref[...].astype(BF16), w_ref[...])
    half = N_KV_HEADS * HEAD_DIM
    k_ref[...] = kv[:, :half]
    v_ref[...] = kv[:, half:]


def _kv_proj(h, w_kv):
    t = h.shape[0]
    half = N_KV_HEADS * HEAD_DIM
    return pl.pallas_call(
        _kv_kernel,
        grid=(t // KV_TOKENS,),
        in_specs=[
            pl.BlockSpec((KV_TOKENS, D_MODEL), lambda i: (i, 0)),
            _const_spec((D_MODEL, 2 * half)),
        ],
        out_specs=[
            pl.BlockSpec((KV_TOKENS, half), lambda i: (i, 0)),
            pl.BlockSpec((KV_TOKENS, half), lambda i: (i, 0)),
        ],
        out_shape=[jax.ShapeDtypeStruct((t, half), F32)] * 2,
        compiler_params=_params("arbitrary"),
        name="kv_proj",
    )(h, w_kv)


def _sink_softmax_rows(s, valid, sink):
    s = jnp.where(valid, s, NEG_INF)
    m = jnp.maximum(jnp.max(s, axis=0, keepdims=True), sink)
    p = jnp.exp(s - m)
    den = jnp.sum(p, axis=0, keepdims=True) + jnp.exp(sink - m)
    return p / den


def _alibi_slope(h):
    return 2.0 ** (-8.0 * (h + 1) / N_HEADS)


def _attn_prompt_kernel(sink_ref, h_ref, kp_ref, kc_ref, vp_ref, vc_ref, wqT_ref, woT_ref,
                        g1_ref, b1_ref, o_ref, oT_scr, *, is_first):
    hb = h_ref[...]
    qT = _dot_nt(wqT_ref[...], hb.astype(BF16)).astype(BF16)
    kk = jnp.concatenate([kp_ref[...], kc_ref[...]], axis=0).astype(BF16)
    vv = jnp.concatenate([vp_ref[...], vc_ref[...]], axis=0)
    vvT = vv.T.astype(BF16)

    kj = lax.broadcasted_iota(jnp.int32, (2 * WINDOW, WINDOW), 0)
    qi = lax.broadcasted_iota(jnp.int32, (2 * WINDOW, WINDOW), 1)
    dist = qi + WINDOW - kj
    valid = (dist >= 0) & (dist < WINDOW) & (kj >= is_first * WINDOW)
    distf = dist.astype(F32)

    zeros_q = jnp.zeros((HEAD_DIM, GROUP * WINDOW), BF16)
    for k in range(N_KV_HEADS):
        heads = range(k * GROUP, (k + 1) * GROUP)
        q_grp = jnp.concatenate([qT[h * HEAD_DIM:(h + 1) * HEAD_DIM] for h in heads], axis=1)
        q_pad = jnp.concatenate([q_grp, zeros_q] if k == 0 else [zeros_q, q_grp], axis=0)
        s_all = _dot(kk, q_pad) * (HEAD_DIM ** -0.5)
        p_blocks = []
        for g, h in enumerate(heads):
            s = s_all[:, g * WINDOW:(g + 1) * WINDOW] - _alibi_slope(h) * distf
            p_blocks.append(_sink_softmax_rows(s, valid, sink_ref[h]).astype(BF16))
        o_all = _dot(vvT[k * HEAD_DIM:(k + 1) * HEAD_DIM],
                     jnp.concatenate(p_blocks, axis=1))
        for g, h in enumerate(heads):
            oT_scr[h * HEAD_DIM:(h + 1) * HEAD_DIM, :] = o_all[:, g * WINDOW:(g + 1) * WINDOW]
    outT = _dot(woT_ref[...], oT_scr[...].astype(BF16))
    o_ref[...] = _layer_norm(DEEPNORM_ALPHA * hb + outT.T, g1_ref[...], b1_ref[...])


def _attn_sample_kernel(sink_ref, h_ref, ck_ref, kn_ref, cv_ref, vn_ref, wqT_ref, woT_ref,
                        g1_ref, b1_ref, o_ref, oT_scr, *, dec_seq):
    n_seq = WINDOW // dec_seq
    n_cache = n_seq * WINDOW
    n_keys = n_cache + WINDOW
    hb = h_ref[...]
    qT = _dot_nt(wqT_ref[...], hb.astype(BF16)).astype(BF16)
    kk = jnp.concatenate([ck_ref[...], kn_ref[...]], axis=0).astype(BF16)
    vv = jnp.concatenate([cv_ref[...], vn_ref[...]], axis=0)
    vvT = vv.T.astype(BF16)

    kidx = lax.broadcasted_iota(jnp.int32, (n_keys, WINDOW), 0)
    q = lax.broadcasted_iota(jnp.int32, (n_keys, WINDOW), 1)
    seq_shift, win_shift = dec_seq.bit_length() - 1, WINDOW.bit_length() - 1
    q_seq, q_pos = q >> seq_shift, q & (dec_seq - 1)
    is_cache = kidx < n_cache
    new_idx = jnp.maximum(kidx - n_cache, 0)
    k_seq = jnp.where(is_cache, kidx >> win_shift, new_idx >> seq_shift)
    dist = jnp.where(is_cache, q_pos + WINDOW - (kidx & (WINDOW - 1)),
                     q_pos - (new_idx & (dec_seq - 1)))
    valid = (k_seq == q_seq) & (dist >= 0) & (dist < WINDOW)
    distf = dist.astype(F32)

    zeros_q = jnp.zeros((HEAD_DIM, WINDOW), BF16)
    for h in range(N_HEADS):
        k = h // GROUP
        q_h = qT[h * HEAD_DIM:(h + 1) * HEAD_DIM]
        q_pad = jnp.concatenate([q_h, zeros_q] if k == 0 else [zeros_q, q_h], axis=0)
        s = _dot(kk, q_pad) * (HEAD_DIM ** -0.5) - _alibi_slope(h) * distf
        p = _sink_softmax_rows(s, valid, sink_ref[h]).astype(BF16)
        oT_scr[h * HEAD_DIM:(h + 1) * HEAD_DIM, :] = _dot(vvT[k * HEAD_DIM:(k + 1) * HEAD_DIM], p)
    outT = _dot(woT_ref[...], oT_scr[...].astype(BF16))
    o_ref[...] = _layer_norm(DEEPNORM_ALPHA * hb + outT.T, g1_ref[...], b1_ref[...])


def _attn_layer(h, k_all, v_all, cache_k, cache_v, sinks, wqT, woT, g1, b1,
                *, n_prompt_tokens, blocks_per_seq, dec_seq):
    t = h.shape[0]
    half = N_KV_HEADS * HEAD_DIM
    rows_per_step = ATTN_BLOCKS * WINDOW
    n_prompt_steps = n_prompt_tokens // rows_per_step
    n_sample_steps = (t - n_prompt_tokens) // rows_per_step
    assert blocks_per_seq % ATTN_BLOCKS == 0
    n_seq = WINDOW // dec_seq
    cur_map = lambda i: (i, 0)
    prev_map = lambda i: (jnp.maximum(i * ATTN_BLOCKS - 1, 0), 0)
    cache_map = lambda i: (jnp.clip(i - n_prompt_steps, 0, n_sample_steps - 1), 0)

    def body(sink_ref, h_ref, kp_ref, kc_ref, vp_ref, vc_ref, ck_ref, cv_ref,
             wqT_ref, woT_ref, g1_ref, b1_ref, o_ref, oT_scr):
        step = pl.program_id(0)
        is_prompt = step < n_prompt_steps
        weights = (wqT_ref, woT_ref, g1_ref, b1_ref)

        def blk(ref, s, rows=WINDOW):
            return ref.at[s * rows:(s + 1) * rows]

        @pl.when(is_prompt)
        def _():
            opens_seq = (step % (blocks_per_seq // ATTN_BLOCKS) == 0).astype(jnp.int32)
            for s in range(ATTN_BLOCKS):
                k_prev = kp_ref if s == 0 else blk(kc_ref, s - 1)
                v_prev = vp_ref if s == 0 else blk(vc_ref, s - 1)
                _attn_prompt_kernel(sink_ref, blk(h_ref, s), k_prev, blk(kc_ref, s),
                                    v_prev, blk(vc_ref, s), *weights, blk(o_ref, s),
                                    oT_scr.at[s], is_first=opens_seq if s == 0 else 0)

        @pl.when(jnp.logical_not(is_prompt))
        def _():
            for s in range(ATTN_BLOCKS):
                _attn_sample_kernel(sink_ref, blk(h_ref, s), blk(ck_ref, s, n_seq * WINDOW),
                                    blk(kc_ref, s), blk(cv_ref, s, n_seq * WINDOW),
                                    blk(vc_ref, s), *weights, blk(o_ref, s), oT_scr.at[s],
                                    dec_seq=dec_seq)

    return pl.pallas_call(
        body,
        grid=(n_prompt_steps + n_sample_steps,),
        in_specs=[
            pl.BlockSpec(memory_space=pltpu.SMEM),
            pl.BlockSpec((rows_per_step, D_MODEL), cur_map),
            pl.BlockSpec((WINDOW, half), prev_map),
            pl.BlockSpec((rows_per_step, half), cur_map),
            pl.BlockSpec((WINDOW, half), prev_map),
            pl.BlockSpec((rows_per_step, half), cur_map),
            pl.BlockSpec((ATTN_BLOCKS * n_seq * WINDOW, half), cache_map),
            pl.BlockSpec((ATTN_BLOCKS * n_seq * WINDOW, half), cache_map),
            _const_spec((D_MODEL, D_MODEL)),
            _const_spec((D_MODEL, D_MODEL)),
            _const_spec((1, D_MODEL)),
            _const_spec((1, D_MODEL)),
        ],
        out_specs=pl.BlockSpec((rows_per_step, D_MODEL), cur_map),
        out_shape=jax.ShapeDtypeStruct((t, D_MODEL), F32),
        scratch_shapes=[pltpu.VMEM((ATTN_BLOCKS, D_MODEL, WINDOW), F32)],
        compiler_params=_params("arbitrary"),
        name="attn_layer",
    )(sinks, h, k_all, k_all, v_all, v_all, cache_k, cache_v, wqT, woT, g1, b1)


NOT_RANKED = 127.0


def _top16_rows(s):
    n = N_KEYS // 8
    v = [s[8 * i:8 * (i + 1)] for i in range(n)]
    k = 2
    while k <= n:
        j = k // 2
        while j >= 1:
            for i in range(n):
                l = i ^ j
                if l > i:
                    hi, lo = jnp.maximum(v[i], v[l]), jnp.minimum(v[i], v[l])
                    v[i], v[l] = (hi, lo) if (i & k) == 0 else (lo, hi)
            j //= 2
        k *= 2
    row_id = lax.broadcasted_iota(jnp.int32, (PEER_TOPK, LANES), 0)
    rows = []
    stacked = jnp.full((PEER_TOPK, LANES), NEG_INF, F32)
    for i in range(PEER_TOPK):
        m = jnp.max(v[0], axis=0, keepdims=True)
        rows.append(m)
        stacked = jnp.where(row_id == i, m, stacked)
        hit = v[0] == m
        for d in range(PEER_TOPK - 1 - i):
            v[d] = jnp.where(hit, v[d + 1], v[d])
    return rows, stacked


def _rank_of(s, rows):
    rank = jnp.full(s.shape, NOT_RANKED, F32)
    for i in reversed(range(len(rows))):
        rank = jnp.where(s >= rows[i], float(i), rank)
    return rank


def _pair_threshold(r1, t1, r2, t2):
    row8 = lax.broadcasted_iota(jnp.int32, (8, LANES), 0)
    cands = []
    for i in range(8):
        n = PEER_TOPK // (i + 1)
        c = r1[i] + t2[0:8]
        if n < 8:
            c = jnp.where(row8 < n, c, NEG_INF)
        cands.append(c)
    cands.append(r1[0] + t2[8:16])
    cands.append(t1[8:16] + r2[0])
    top = r1[0] + r2[0]
    z = jnp.zeros((1, LANES), F32)
    m = top
    for _ in range(PEER_TOPK):
        m = cands[0]
        for c in cands[1:]:
            m = jnp.maximum(m, c)
        m = jnp.max(m, axis=0, keepdims=True)
        z = z + jnp.exp(m - top)
        cands = [jnp.where(c == m, NEG_INF, c) for c in cands]
    return m, z


def _peer_select(x_ref, wqT_ref, keys_ref, xb_scr, qT_scr, s1_scr, s2_scr,
                 cnt_scr, e1_scr, rank2_scr, e2_scr):
    xb_scr[...] = x_ref[...].T.astype(BF16)
    qT_scr[...] = _dot(wqT_ref[...], xb_scr[...]).astype(BF16)
    for h in range(PEER_HEADS):
        for p, dst in ((0, s1_scr), (1, s2_scr)):
            r0 = (h * 2 + p) * (D_KEY // 2)
            dst[h] = _dot(keys_ref[p], qT_scr[r0:r0 + D_KEY // 2, :])

    def chunk_body(c, carry):
        lanes = pl.ds(pl.multiple_of(c * LANES, LANES), LANES)

        def one_head(h):
            s1 = s1_scr[h, :, lanes]
            s2 = s2_scr[h, :, lanes]
            r1, t1 = _top16_rows(s1)
            r2, t2 = _top16_rows(s2)
            tau, z = _pair_threshold(r1, t1, r2, t2)
            cnt_by_rank = jnp.zeros(t1.shape, F32)
            for jx in range(PEER_TOPK):
                cnt_by_rank = cnt_by_rank + jnp.where(t1 + r2[jx] >= tau, 1.0, 0.0)
            cnt = jnp.zeros(s1.shape, F32)
            for i in reversed(range(PEER_TOPK)):
                cnt = jnp.where(s1 >= r1[i], cnt_by_rank[i:i + 1], cnt)
            cnt_scr[h, :, lanes] = cnt
            e1_scr[h, :, lanes] = jnp.exp(s1 - r1[0]) * (1.0 / z)
            rank2_scr[h, :, lanes] = _rank_of(s2, r2).astype(BF16)
            e2_scr[h, :, lanes] = jnp.exp(s2 - r2[0]).astype(BF16)

        def head_pair_body(hp, carry2):
            one_head(2 * hp)
            one_head(2 * hp + 1)
            return carry2

        return lax.fori_loop(0, PEER_HEADS // 2, head_pair_body, carry)

    lax.fori_loop(0, PEER_TOKENS // LANES, chunk_body, 0)


def _peer_gate(j, cnt_scr, e1_scr, rank2_scr, e2_scr, w_scr):
    a_per_step = PEER_EXPERTS // N_KEYS
    a_rows = pl.ds(pl.multiple_of(j * a_per_step, a_per_step), a_per_step)
    zero = jnp.zeros((), BF16)
    for c in range(PEER_TOKENS // LANES):
        lanes = slice(c * LANES, (c + 1) * LANES)
        cnt_blk = [cnt_scr[h, a_rows, lanes] for h in range(PEER_HEADS)]
        e1_blk = [e1_scr[h, a_rows, lanes] for h in range(PEER_HEADS)]
        for al in range(a_per_step):
            w = None
            for h in range(PEER_HEADS):
                cnt = jnp.broadcast_to(cnt_blk[h][al:al + 1], (N_KEYS, LANES)).astype(BF16)
                e1 = jnp.broadcast_to(e1_blk[h][al:al + 1], (N_KEYS, LANES)).astype(BF16)
                wh = jnp.where(rank2_scr[h, :, lanes] < cnt, e2_scr[h, :, lanes] * e1, zero)
                w = wh if w is None else w + wh
            w_scr[al * N_KEYS:(al + 1) * N_KEYS, lanes] = w


def _peer_kernel(x_ref, wqT_ref, keys_ref, u_ref, vT_ref, o_ref,
                 xb_scr, qT_scr, s1_scr, s2_scr, cnt_scr, e1_scr, rank2_scr, e2_scr,
                 w_scr, hid_scr, acc_scr):
    j = pl.program_id(1)

    @pl.when(j == 0)
    def _start():
        _peer_select(x_ref, wqT_ref, keys_ref, xb_scr, qT_scr, s1_scr, s2_scr,
                     cnt_scr, e1_scr, rank2_scr, e2_scr)
        acc_scr[...] = jnp.zeros_like(acc_scr)

    hid_scr[...] = _dot(u_ref[0], xb_scr[...]).astype(BF16)
    _peer_gate(j, cnt_scr, e1_scr, rank2_scr, e2_scr, w_scr)
    total = None
    for k in range(PEER_EXPERTS // MXU_DIM):
        rows = slice(k * MXU_DIM, (k + 1) * MXU_DIM)
        g = jax.nn.gelu(hid_scr[rows, :]) * w_scr[rows, :]
        part = _dot(vT_ref[0, :, rows], g)
        total = part if total is None else total + part
    acc_scr[...] += total

    @pl.when(j == pl.num_programs(1) - 1)
    def _finish():
        o_ref[...] = acc_scr[...].T


def _peer_layer(h, wqT, keys, u_tabs, vT_tabs, layer):
    t = h.shape[0]
    n_tiles = N_EXPERTS // PEER_EXPERTS
    return pl.pallas_call(
        _peer_kernel,
        grid=(t // PEER_TOKENS, n_tiles),
        in_specs=[
            pl.BlockSpec((PEER_TOKENS, D_MODEL), lambda i, j: (i, 0)),
            _const_spec((PEER_HEADS * D_KEY, D_MODEL)),
            _const_spec((2, N_KEYS, D_KEY // 2)),
            pl.BlockSpec((1, PEER_EXPERTS, D_MODEL), lambda i, j: (layer, j, 0)),
            pl.BlockSpec((1, D_MODEL, PEER_EXPERTS), lambda i, j: (layer, 0, j)),
        ],
        out_specs=pl.BlockSpec((PEER_TOKENS, D_MODEL), lambda i, j: (i, 0)),
        out_shape=jax.ShapeDtypeStruct((t, D_MODEL), F32),
        scratch_shapes=[
            pltpu.VMEM((D_MODEL, PEER_TOKENS), BF16),
            pltpu.VMEM((PEER_HEADS * D_KEY, PEER_TOKENS), BF16),
            pltpu.VMEM((PEER_HEADS, N_KEYS, PEER_TOKENS), F32),
            pltpu.VMEM((PEER_HEADS, N_KEYS, PEER_TOKENS), F32),
            pltpu.VMEM((PEER_HEADS, N_KEYS, PEER_TOKENS), F32),
            pltpu.VMEM((PEER_HEADS, N_KEYS, PEER_TOKENS), F32),
            pltpu.VMEM((PEER_HEADS, N_KEYS, PEER_TOKENS), BF16),
            pltpu.VMEM((PEER_HEADS, N_KEYS, PEER_TOKENS), BF16),
            pltpu.VMEM((PEER_EXPERTS, PEER_TOKENS), BF16),
            pltpu.VMEM((PEER_EXPERTS, PEER_TOKENS), BF16),
            pltpu.VMEM((D_MODEL, PEER_TOKENS), F32),
        ],
        compiler_params=_params("arbitrary", "arbitrary"),
        name="peer_layer",
    )(h, wqT, keys, u_tabs, vT_tabs)


def _post_kernel(h_ref, y_ref, p_ref, g2_ref, b2_ref, wg_ref, bg_ref, wp_ref, o_ref):
    hn = _layer_norm(DEEPNORM_ALPHA * h_ref[...] + y_ref[...], g2_ref[...], b2_ref[...])
    gate = jax.nn.sigmoid(_dot(hn.astype(BF16), wg_ref[...]) + bg_ref[...])
    o_ref[...] = hn + gate * _dot(p_ref[0].astype(BF16), wp_ref[...])


def _post_layer(h, y, p_prompt, p_sample, layer, g2, b2, wg, bg, wp, *, split_output):
    t = h.shape[0]
    tp = p_prompt.shape[1]
    n_prompt_tiles = tp // POST_TOKENS
    tok = lambda i: (i, 0)
    prompt_tile = lambda i: jnp.minimum(i, n_prompt_tiles - 1)
    sample_tile = lambda i: jnp.maximum(i - n_prompt_tiles, 0)

    def body(h_ref, y_ref, pp_ref, ps_ref, g2_ref, b2_ref, wg_ref, bg_ref, wp_ref, *outs):
        is_prompt = pl.program_id(0) < n_prompt_tiles
        weights = (g2_ref, b2_ref, wg_ref, bg_ref, wp_ref)

        @pl.when(is_prompt)
        def _():
            _post_kernel(h_ref, y_ref, pp_ref, *weights, outs[0])

        @pl.when(jnp.logical_not(is_prompt))
        def _():
            _post_kernel(h_ref, y_ref, ps_ref, *weights, outs[-1])

    if split_output:
        out_specs = [pl.BlockSpec((POST_TOKENS, D_MODEL), lambda i: (prompt_tile(i), 0)),
                     pl.BlockSpec((POST_TOKENS, D_MODEL), lambda i: (sample_tile(i), 0))]
        out_shape = [jax.ShapeDtypeStruct((tp, D_MODEL), F32),
                     jax.ShapeDtypeStruct((t - tp, D_MODEL), F32)]
    else:
        out_specs = pl.BlockSpec((POST_TOKENS, D_MODEL), tok)
        out_shape = jax.ShapeDtypeStruct((t, D_MODEL), F32)
    return pl.pallas_call(
        body,
        grid=(t // POST_TOKENS,),
        in_specs=[
            pl.BlockSpec((POST_TOKENS, D_MODEL), tok),
            pl.BlockSpec((POST_TOKENS, D_MODEL), tok),
            pl.BlockSpec((1, POST_TOKENS, D_PLE), lambda i: (layer, prompt_tile(i), 0)),
            pl.BlockSpec((1, POST_TOKENS, D_PLE), lambda i: (layer, sample_tile(i), 0)),
            _const_spec((1, D_MODEL)),
            _const_spec((1, D_MODEL)),
            _const_spec((D_MODEL, D_MODEL)),
            _const_spec((1, D_MODEL)),
            _const_spec((D_PLE, D_MODEL)),
        ],
        out_specs=out_specs,
        out_shape=out_shape,
        compiler_params=_params("arbitrary"),
        name="post_layer",
    )(h, y, p_prompt, p_sample, g2, b2, wg, bg, wp)


def _row(v):
    return v.reshape(1, -1).astype(F32)


def _sgu_spatial_params(w_s, b_s, dec_seq):
    tri = jnp.tril(jnp.ones((CHUNK, CHUNK), bool))
    ws_prompt = jnp.where(tri, w_s, 0.0)
    tri_s = jnp.tril(jnp.ones((dec_seq, dec_seq), bool))
    small = jnp.where(tri_s, w_s[:, :dec_seq, :dec_seq], 0.0)
    eye = jnp.eye(CHUNK // dec_seq, dtype=w_s.dtype)
    ws_sample = jnp.einsum("ab,gts->gatbs", eye, small).reshape(SGU_GROUPS, CHUNK, CHUNK)
    ws2 = jnp.stack([ws_prompt, ws_sample]).astype(BF16)
    bs_prompt = b_s.T
    bs_sample = jnp.tile(b_s[:, :dec_seq].T, (CHUNK // dec_seq, 1))
    bs2 = jnp.stack([bs_prompt, bs_sample])
    bs2 = jnp.repeat(bs2, SGU_GROUP_DIM, axis=2).astype(F32)
    return ws2, bs2


def kernel(x_prompt, x_sample, cache_k_win, cache_v_win, p_prompt, p_sample,
           ln1_g, ln1_b, ln2_g, ln2_b,
           sgu_w_in, sgu_b_in, sgu_ln_g, sgu_ln_b, sgu_w_s, sgu_b_s, sgu_w_out, sgu_b_out,
           attn_w_kv, attn_w_q, attn_sinks, attn_w_o,
           peer_w_q, peer_subkeys, peer_u, peer_v,
           ple_w, ple_gate_w, ple_gate_b):
    batch, seq, _ = x_prompt.shape
    dec_batch, dec_seq, _ = x_sample.shape
    tp = batch * seq
    ts = dec_batch * dec_seq
    half = N_KV_HEADS * HEAD_DIM
    wc = cache_k_win.shape[1]
    assert wc == WINDOW and CHUNK % dec_seq == 0 and seq % WINDOW == 0
    assert dec_seq & (dec_seq - 1) == 0 and WINDOW & (WINDOW - 1) == 0
    assert tp % PEER_TOKENS == 0 and ts % PEER_TOKENS == 0 and ts % SGU_TOKENS == 0

    h = jnp.concatenate([x_prompt.reshape(tp, D_MODEL), x_sample.reshape(ts, D_MODEL)], axis=0)
    pp = p_prompt.reshape(DEPTH, tp, D_PLE)
    ps = p_sample.reshape(DEPTH, ts, D_PLE)
    ck = cache_k_win.reshape(dec_batch * wc, half).astype(F32)
    cv = cache_v_win.reshape(dec_batch * wc, half).astype(F32)
    u_tabs = peer_u.astype(BF16)
    vT_tabs = jnp.swapaxes(peer_v, 1, 2).astype(BF16)

    sgu_rows = []
    k_all = v_all = None
    for i in range(DEPTH):
        g1, b1 = _row(ln1_g[i]), _row(ln1_b[i])
        if i < N_A_LAYERS:
            ws2, bs2 = _sgu_spatial_params(sgu_w_s[i], sgu_b_s[i], dec_seq)
            h, v_rows = _sgu_layer(
                h, tp, sgu_w_in[i].astype(BF16), _row(sgu_b_in[i]), _row(sgu_ln_g[i]),
                _row(sgu_ln_b[i]), ws2, bs2, sgu_w_out[i].astype(BF16), _row(sgu_b_out[i]), g1, b1)
            sgu_rows.append(v_rows.reshape(dec_batch, dec_seq, D_SGU))
        else:
            if i == N_A_LAYERS:
                k_all, v_all = _kv_proj(h, attn_w_kv.astype(BF16))
            jl = i - N_A_LAYERS
            wqT = attn_w_q[jl].T.astype(BF16)
            woT = attn_w_o[jl].T.astype(BF16)
            sinks = attn_sinks[jl].astype(F32)
            h = _attn_layer(h, k_all, v_all, ck, cv, sinks, wqT, woT, g1, b1,
                            n_prompt_tokens=tp, blocks_per_seq=seq // WINDOW, dec_seq=dec_seq)
        y = _peer_layer(h, peer_w_q[i].T.astype(BF16), peer_subkeys[i].astype(BF16),
                        u_tabs, vT_tabs, i)
        h = _post_layer(h, y, pp, ps, i, _row(ln2_g[i]), _row(ln2_b[i]),
                        ple_gate_w[i].astype(BF16), _row(ple_gate_b[i]), ple_w[i].astype(BF16),
                        split_output=(i == DEPTH - 1))

    y_prompt = h[0].reshape(batch, seq, D_MODEL)
    y_sample = h[1].reshape(dec_batch, dec_seq, D_MODEL)
    kp4 = k_all[:tp].reshape(batch, seq, N_KV_HEADS, HEAD_DIM)
    vp4 = v_all[:tp].reshape(batch, seq, N_KV_HEADS, HEAD_DIM)
    new_k_win_prompt = kp4[:, -WINDOW:]
    new_v_win_prompt = vp4[:, -WINDOW:]
    kn4 = k_all[tp:].reshape(dec_batch, dec_seq, N_KV_HEADS, HEAD_DIM)
    vn4 = v_all[tp:].reshape(dec_batch, dec_seq, N_KV_HEADS, HEAD_DIM)
    new_k_win_sample = jnp.concatenate([cache_k_win.astype(F32), kn4], axis=1)[:, -wc:]
    new_v_win_sample = jnp.concatenate([cache_v_win.astype(F32), vn4], axis=1)[:, -wc:]
    new_sgu_v_sample = jnp.stack(sgu_rows, axis=0)
    return (y_prompt, y_sample, new_k_win_prompt, new_v_win_prompt,
            new_k_win_sample, new_v_win_sample, new_sgu_v_sample)
```

```python
import functools

import jax
import jax.numpy as jnp
from jax import lax
from jax.experimental import pallas as pl
from jax.experimental.pallas import tpu as pltpu

D_MODEL = 1024
DEPTH = 4
N_A_LAYERS = DEPTH // 2
CHUNK = 128
D_SGU = 2 * D_MODEL
SGU_GROUPS = 8
SGU_GROUP_DIM = D_SGU // SGU_GROUPS
HEAD_DIM = 64
N_HEADS = D_MODEL // HEAD_DIM
N_KV_HEADS = 2
GROUP = N_HEADS // N_KV_HEADS
WINDOW = 128
PEER_HEADS = 8
N_KEYS = 128
N_EXPERTS = N_KEYS * N_KEYS
D_KEY = 256
PEER_TOPK = 16
D_PLE = 256
DEEPNORM_ALPHA = (2.0 * DEPTH) ** 0.25
LN_EPS = 1e-5

LANES = 128
MXU_DIM = 256
VMEM_LIMIT_BYTES = 56 * 1024 * 1024

SGU_TOKENS = 256
PEER_TOKENS = 512
PEER_EXPERTS = 2048
POST_TOKENS = 512
KV_TOKENS = 512

NEG_INF = float("-inf")
NT_DIMS = (((1,), (1,)), ((), ()))

BF16 = jnp.bfloat16
F32 = jnp.float32


def _layer_norm(x, g, b):
    mu = jnp.mean(x, axis=-1, keepdims=True)
    xc = x - mu
    var = jnp.mean(xc * xc, axis=-1, keepdims=True)
    return xc * lax.rsqrt(var + LN_EPS) * g + b


def _dot(a, b):
    return jnp.dot(a, b, preferred_element_type=F32)


def _dot_nt(a, b):
    return lax.dot_general(a, b, NT_DIMS, preferred_element_type=F32)


def _const_spec(shape):
    n = len(shape)
    return pl.BlockSpec(shape, lambda *_: (0,) * n)


def _params(*semantics):
    return pltpu.CompilerParams(dimension_semantics=semantics,
                                vmem_limit_bytes=VMEM_LIMIT_BYTES)


def _sgu_kernel(x_ref, win_ref, bin_ref, lng_ref, lnb_ref, ws_ref, bs_ref,
                wout_ref, bout_ref, g1_ref, b1_ref, h_ref, v_ref,
                u_scr, v_scr):
    x = x_ref[...]
    xb = x.astype(BF16)
    n_col = (2 * D_SGU) // 512
    for j in range(n_col):
        z = _dot(xb, win_ref[:, j * 512:(j + 1) * 512]) + bin_ref[:, j * 512:(j + 1) * 512]
        z = jax.nn.gelu(z)
        if j < n_col // 2:
            u_scr[:, j * 512:(j + 1) * 512] = z
        else:
            jj = j - n_col // 2
            v_scr[:, jj * 512:(jj + 1) * 512] = z
    v = _layer_norm(v_scr[...], lng_ref[...], lnb_ref[...])
    v_ref[...] = v
    v_scr[...] = v
    for c in range(SGU_TOKENS // CHUNK):
        rows = slice(c * CHUNK, (c + 1) * CHUNK)
        for g in range(SGU_GROUPS):
            cols = slice(g * SGU_GROUP_DIM, (g + 1) * SGU_GROUP_DIM)
            mixed = _dot(ws_ref[0, g], v_scr[rows, cols].astype(BF16)) + bs_ref[0, :, cols]
            u_scr[rows, cols] = u_scr[rows, cols] * mixed
    out = _dot(u_scr[...].astype(BF16), wout_ref[...]) + bout_ref[...]
    h_ref[...] = _layer_norm(DEEPNORM_ALPHA * x + out, g1_ref[...], b1_ref[...])


def _sgu_layer(h, n_prompt_tokens, win, b_in, ln_g, ln_b, ws2, bs2, wout, b_out, g1, b1):
    t = h.shape[0]
    n_tiles = t // SGU_TOKENS
    n_prompt_tiles = n_prompt_tokens // SGU_TOKENS
    t_sample = t - n_prompt_tokens

    def kind(i):
        return jnp.where(i < n_prompt_tiles, 0, 1)

    return pl.pallas_call(
        _sgu_kernel,
        grid=(n_tiles,),
        in_specs=[
            pl.BlockSpec((SGU_TOKENS, D_MODEL), lambda i: (i, 0)),
            _const_spec((D_MODEL, 2 * D_SGU)),
            _const_spec((1, 2 * D_SGU)),
            _const_spec((1, D_SGU)),
            _const_spec((1, D_SGU)),
            pl.BlockSpec((1, SGU_GROUPS, CHUNK, CHUNK), lambda i: (kind(i), 0, 0, 0)),
            pl.BlockSpec((1, CHUNK, D_SGU), lambda i: (kind(i), 0, 0)),
            _const_spec((D_SGU, D_MODEL)),
            _const_spec((1, D_MODEL)),
            _const_spec((1, D_MODEL)),
            _const_spec((1, D_MODEL)),
        ],
        out_specs=[
            pl.BlockSpec((SGU_TOKENS, D_MODEL), lambda i: (i, 0)),
            pl.BlockSpec((SGU_TOKENS, D_SGU), lambda i: (jnp.maximum(i - n_prompt_tiles, 0), 0)),
        ],
        out_shape=[
            jax.ShapeDtypeStruct((t, D_MODEL), F32),
            jax.ShapeDtypeStruct((t_sample, D_SGU), F32),
        ],
        scratch_shapes=[
            pltpu.VMEM((SGU_TOKENS, D_SGU), F32),
            pltpu.VMEM((SGU_TOKENS, D_SGU), F32),
        ],
        compiler_params=_params("arbitrary"),
        name="sgu_layer",
    )(h, win, b_in, ln_g, ln_b, ws2, bs2, wout, b_out, g1, b1)


def _kv_kernel(h_ref, w_ref, k_ref, v_ref):
    kv = _dot(h_ref[...].astype(BF16), w_ref[...])
    half = N_KV_HEADS * HEAD_DIM
    k_ref[...] = kv[:, :half]
    v_ref[...] = kv[:, half:]


def _kv_proj(h, w_kv):
    t = h.shape[0]
    half = N_KV_HEADS * HEAD_DIM
    return pl.pallas_call(
        _kv_kernel,
        grid=(t // KV_TOKENS,),
        in_specs=[
            pl.BlockSpec((KV_TOKENS, D_MODEL), lambda i: (i, 0)),
            _const_spec((D_MODEL, 2 * half)),
        ],
        out_specs=[
            pl.BlockSpec((KV_TOKENS, half), lambda i: (i, 0)),
            pl.BlockSpec((KV_TOKENS, half), lambda i: (i, 0)),
        ],
        out_shape=[jax.ShapeDtypeStruct((t, half), F32)] * 2,
        compiler_params=_params("arbitrary"),
        name="kv_proj",
    )(h, w_kv)


def _sink_softmax_rows(s, valid, sink):
    s = jnp.where(valid, s, NEG_INF)
    m = jnp.maximum(jnp.max(s, axis=0, keepdims=True), sink)
    p = jnp.exp(s - m)
    den = jnp.sum(p, axis=0, keepdims=True) + jnp.exp(sink - m)
    return p / den


def _alibi_slope(h):
    return 2.0 ** (-8.0 * (h + 1) / N_HEADS)


def _attn_prompt_kernel(sink_ref, h_ref, kp_ref, kc_ref, vp_ref, vc_ref, wqT_ref, woT_ref,
                        g1_ref, b1_ref, o_ref, oT_scr, *, blocks_per_seq):
    hb = h_ref[...]
    qT = _dot_nt(wqT_ref[...], hb.astype(BF16)).astype(BF16)
    kk = jnp.concatenate([kp_ref[...], kc_ref[...]], axis=0).astype(BF16)
    vv = jnp.concatenate([vp_ref[...], vc_ref[...]], axis=0)
    vvT = vv.T.astype(BF16)

    kj = lax.broadcasted_iota(jnp.int32, (2 * WINDOW, WINDOW), 0)
    qi = lax.broadcasted_iota(jnp.int32, (2 * WINDOW, WINDOW), 1)
    dist = qi + WINDOW - kj
    is_first = (pl.program_id(0) % blocks_per_seq == 0).astype(jnp.int32)
    valid = (dist >= 0) & (dist < WINDOW) & (kj >= is_first * WINDOW)
    distf = dist.astype(F32)

    zeros_q = jnp.zeros((HEAD_DIM, GROUP * WINDOW), BF16)
    for k in range(N_KV_HEADS):
        heads = range(k * GROUP, (k + 1) * GROUP)
        q_grp = jnp.concatenate([qT[h * HEAD_DIM:(h + 1) * HEAD_DIM] for h in heads], axis=1)
        q_pad = jnp.concatenate([q_grp, zeros_q] if k == 0 else [zeros_q, q_grp], axis=0)
        s_all = _dot(kk, q_pad) * (HEAD_DIM ** -0.5)
        p_blocks = []
        for g, h in enumerate(heads):
            s = s_all[:, g * WINDOW:(g + 1) * WINDOW] - _alibi_slope(h) * distf
            p_blocks.append(_sink_softmax_rows(s, valid, sink_ref[h]).astype(BF16))
        o_all = _dot(vvT[k * HEAD_DIM:(k + 1) * HEAD_DIM],
                     jnp.concatenate(p_blocks, axis=1))
        for g, h in enumerate(heads):
            oT_scr[h * HEAD_DIM:(h + 1) * HEAD_DIM, :] = o_all[:, g * WINDOW:(g + 1) * WINDOW]
    outT = _dot(woT_ref[...], oT_scr[...].astype(BF16))
    o_ref[...] = _layer_norm(DEEPNORM_ALPHA * hb + outT.T, g1_ref[...], b1_ref[...])


def _attn_sample_kernel(sink_ref, h_ref, ck_ref, kn_ref, cv_ref, vn_ref, wqT_ref, woT_ref,
                        g1_ref, b1_ref, o_ref, oT_scr, *, dec_seq):
    n_seq = WINDOW // dec_seq
    n_cache = n_seq * WINDOW
    n_keys = n_cache + WINDOW
    hb = h_ref[...]
    qT = _dot_nt(wqT_ref[...], hb.astype(BF16)).astype(BF16)
    kk = jnp.concatenate([ck_ref[...], kn_ref[...]], axis=0).astype(BF16)
    vv = jnp.concatenate([cv_ref[...], vn_ref[...]], axis=0)
    vvT = vv.T.astype(BF16)

    kidx = lax.broadcasted_iota(jnp.int32, (n_keys, WINDOW), 0)
    q = lax.broadcasted_iota(jnp.int32, (n_keys, WINDOW), 1)
    seq_shift, win_shift = dec_seq.bit_length() - 1, WINDOW.bit_length() - 1
    q_seq, q_pos = q >> seq_shift, q & (dec_seq - 1)
    is_cache = kidx < n_cache
    new_idx = jnp.maximum(kidx - n_cache, 0)
    k_seq = jnp.where(is_cache, kidx >> win_shift, new_idx >> seq_shift)
    dist = jnp.where(is_cache, q_pos + WINDOW - (kidx & (WINDOW - 1)),
                     q_pos - (new_idx & (dec_seq - 1)))
    valid = (k_seq == q_seq) & (dist >= 0) & (dist < WINDOW)
    distf = dist.astype(F32)

    zeros_q = jnp.zeros((HEAD_DIM, WINDOW), BF16)
    for h in range(N_HEADS):
        k = h // GROUP
        q_h = qT[h * HEAD_DIM:(h + 1) * HEAD_DIM]
        q_pad = jnp.concatenate([q_h, zeros_q] if k == 0 else [zeros_q, q_h], axis=0)
        s = _dot(kk, q_pad) * (HEAD_DIM ** -0.5) - _alibi_slope(h) * distf
        p = _sink_softmax_rows(s, valid, sink_ref[h]).astype(BF16)
        oT_scr[h * HEAD_DIM:(h + 1) * HEAD_DIM, :] = _dot(vvT[k * HEAD_DIM:(k + 1) * HEAD_DIM], p)
    outT = _dot(woT_ref[...], oT_scr[...].astype(BF16))
    o_ref[...] = _layer_norm(DEEPNORM_ALPHA * hb + outT.T, g1_ref[...], b1_ref[...])


def _attn_layer(h, k_all, v_all, cache_k, cache_v, sinks, wqT, woT, g1, b1,
                *, n_prompt_tokens, blocks_per_seq, dec_seq):
    t = h.shape[0]
    half = N_KV_HEADS * HEAD_DIM
    n_prompt_blocks = n_prompt_tokens // WINDOW
    n_sample_blocks = (t - n_prompt_tokens) // WINDOW
    n_seq = WINDOW // dec_seq
    cur_map = lambda i: (i, 0)
    prev_map = lambda i: (jnp.maximum(i - 1, 0), 0)
    cache_map = lambda i: (jnp.clip(i - n_prompt_blocks, 0, n_sample_blocks - 1), 0)

    def body(sink_ref, h_ref, kp_ref, kc_ref, vp_ref, vc_ref, ck_ref, cv_ref, *rest):
        is_prompt = pl.program_id(0) < n_prompt_blocks

        @pl.when(is_prompt)
        def _():
            _attn_prompt_kernel(sink_ref, h_ref, kp_ref, kc_ref, vp_ref, vc_ref, *rest,
                                blocks_per_seq=blocks_per_seq)

        @pl.when(jnp.logical_not(is_prompt))
        def _():
            _attn_sample_kernel(sink_ref, h_ref, ck_ref, kc_ref, cv_ref, vc_ref, *rest,
                                dec_seq=dec_seq)

    return pl.pallas_call(
        body,
        grid=(n_prompt_blocks + n_sample_blocks,),
        in_specs=[
            pl.BlockSpec(memory_space=pltpu.SMEM),
            pl.BlockSpec((WINDOW, D_MODEL), cur_map),
            pl.BlockSpec((WINDOW, half), prev_map),
            pl.BlockSpec((WINDOW, half), cur_map),
            pl.BlockSpec((WINDOW, half), prev_map),
            pl.BlockSpec((WINDOW, half), cur_map),
            pl.BlockSpec((n_seq * WINDOW, half), cache_map),
            pl.BlockSpec((n_seq * WINDOW, half), cache_map),
            _const_spec((D_MODEL, D_MODEL)),
            _const_spec((D_MODEL, D_MODEL)),
            _const_spec((1, D_MODEL)),
            _const_spec((1, D_MODEL)),
        ],
        out_specs=pl.BlockSpec((WINDOW, D_MODEL), cur_map),
        out_shape=jax.ShapeDtypeStruct((t, D_MODEL), F32),
        scratch_shapes=[pltpu.VMEM((D_MODEL, WINDOW), F32)],
        compiler_params=_params("arbitrary"),
        name="attn_layer",
    )(sinks, h, k_all, k_all, v_all, v_all, cache_k, cache_v, wqT, woT, g1, b1)


NOT_RANKED = 127.0


def _top16_rows(s):
    n = N_KEYS // 8
    v = [s[8 * i:8 * (i + 1)] for i in range(n)]
    k = 2
    while k <= n:
        j = k // 2
        while j >= 1:
            for i in range(n):
                l = i ^ j
                if l > i:
                    hi, lo = jnp.maximum(v[i], v[l]), jnp.minimum(v[i], v[l])
                    v[i], v[l] = (hi, lo) if (i & k) == 0 else (lo, hi)
            j //= 2
        k *= 2
    row_id = lax.broadcasted_iota(jnp.int32, (PEER_TOPK, LANES), 0)
    rows = []
    stacked = jnp.full((PEER_TOPK, LANES), NEG_INF, F32)
    for i in range(PEER_TOPK):
        m = jnp.max(v[0], axis=0, keepdims=True)
        rows.append(m)
        stacked = jnp.where(row_id == i, m, stacked)
        hit = v[0] == m
        for d in range(PEER_TOPK - 1 - i):
            v[d] = jnp.where(hit, v[d + 1], v[d])
    return rows, stacked


def _rank_of(s, rows):
    rank = jnp.full(s.shape, NOT_RANKED, F32)
    for i in reversed(range(len(rows))):
        rank = jnp.where(s >= rows[i], float(i), rank)
    return rank


def _pair_threshold(r1, t1, r2, t2):
    row8 = lax.broadcasted_iota(jnp.int32, (8, LANES), 0)
    cands = []
    for i in range(8):
        n = PEER_TOPK // (i + 1)
        c = r1[i] + t2[0:8]
        if n < 8:
            c = jnp.where(row8 < n, c, NEG_INF)
        cands.append(c)
    cands.append(r1[0] + t2[8:16])
    cands.append(t1[8:16] + r2[0])
    top = r1[0] + r2[0]
    z = jnp.zeros((1, LANES), F32)
    m = top
    for _ in range(PEER_TOPK):
        m = cands[0]
        for c in cands[1:]:
            m = jnp.maximum(m, c)
        m = jnp.max(m, axis=0, keepdims=True)
        z = z + jnp.exp(m - top)
        cands = [jnp.where(c == m, NEG_INF, c) for c in cands]
    return m, z


def _peer_select(x_ref, wqT_ref, keys_ref, xb_scr, qT_scr, s1_scr, s2_scr,
                 cnt_scr, e1_scr, rank2_scr, e2_scr):
    xb_scr[...] = x_ref[...].T.astype(BF16)
    qT_scr[...] = _dot(wqT_ref[...], xb_scr[...]).astype(BF16)
    for h in range(PEER_HEADS):
        for p, dst in ((0, s1_scr), (1, s2_scr)):
            r0 = (h * 2 + p) * (D_KEY // 2)
            dst[h] = _dot(keys_ref[p], qT_scr[r0:r0 + D_KEY // 2, :])

    def chunk_body(c, carry):
        lanes = pl.ds(pl.multiple_of(c * LANES, LANES), LANES)

        def one_head(h):
            s1 = s1_scr[h, :, lanes]
            s2 = s2_scr[h, :, lanes]
            r1, t1 = _top16_rows(s1)
            r2, t2 = _top16_rows(s2)
            tau, z = _pair_threshold(r1, t1, r2, t2)
            cnt_by_rank = jnp.zeros(t1.shape, F32)
            for jx in range(PEER_TOPK):
                cnt_by_rank = cnt_by_rank + jnp.where(t1 + r2[jx] >= tau, 1.0, 0.0)
            cnt = jnp.zeros(s1.shape, F32)
            for i in reversed(range(PEER_TOPK)):
                cnt = jnp.where(s1 >= r1[i], cnt_by_rank[i:i + 1], cnt)
            cnt_scr[h, :, lanes] = cnt
            e1_scr[h, :, lanes] = jnp.exp(s1 - r1[0]) * (1.0 / z)
            rank2_scr[h, :, lanes] = _rank_of(s2, r2).astype(BF16)
            e2_scr[h, :, lanes] = jnp.exp(s2 - r2[0]).astype(BF16)

        def head_pair_body(hp, carry2):
            one_head(2 * hp)
            one_head(2 * hp + 1)
            return carry2

        return lax.fori_loop(0, PEER_HEADS // 2, head_pair_body, carry)

    lax.fori_loop(0, PEER_TOKENS // LANES, chunk_body, 0)


def _peer_gate(j, cnt_scr, e1_scr, rank2_scr, e2_scr, w_scr):
    a_per_step = PEER_EXPERTS // N_KEYS
    a_rows = pl.ds(pl.multiple_of(j * a_per_step, a_per_step), a_per_step)
    zero = jnp.zeros((), BF16)
    for c in range(PEER_TOKENS // LANES):
        lanes = slice(c * LANES, (c + 1) * LANES)
        cnt_blk = [cnt_scr[h, a_rows, lanes] for h in range(PEER_HEADS)]
        e1_blk = [e1_scr[h, a_rows, lanes] for h in range(PEER_HEADS)]
        for al in range(a_per_step):
            w = None
            for h in range(PEER_HEADS):
                cnt = jnp.broadcast_to(cnt_blk[h][al:al + 1], (N_KEYS, LANES)).astype(BF16)
                e1 = jnp.broadcast_to(e1_blk[h][al:al + 1], (N_KEYS, LANES)).astype(BF16)
                wh = jnp.where(rank2_scr[h, :, lanes] < cnt, e2_scr[h, :, lanes] * e1, zero)
                w = wh if w is None else w + wh
            w_scr[al * N_KEYS:(al + 1) * N_KEYS, lanes] = w


def _peer_kernel(x_ref, wqT_ref, keys_ref, u_ref, vT_ref, o_ref,
                 xb_scr, qT_scr, s1_scr, s2_scr, cnt_scr, e1_scr, rank2_scr, e2_scr,
                 w_scr, hid_scr, acc_scr):
    j = pl.program_id(1)

    @pl.when(j == 0)
    def _start():
        _peer_select(x_ref, wqT_ref, keys_ref, xb_scr, qT_scr, s1_scr, s2_scr,
                     cnt_scr, e1_scr, rank2_scr, e2_scr)
        acc_scr[...] = jnp.zeros_like(acc_scr)

    hid_scr[...] = _dot(u_ref[0], xb_scr[...]).astype(BF16)
    _peer_gate(j, cnt_scr, e1_scr, rank2_scr, e2_scr, w_scr)
    total = None
    for k in range(PEER_EXPERTS // MXU_DIM):
        rows = slice(k * MXU_DIM, (k + 1) * MXU_DIM)
        g = jax.nn.gelu(hid_scr[rows, :]) * w_scr[rows, :]
        part = _dot(vT_ref[0, :, rows], g)
        total = part if total is None else total + part
    acc_scr[...] += total

    @pl.when(j == pl.num_programs(1) - 1)
    def _finish():
        o_ref[...] = acc_scr[...].T


def _peer_layer(h, wqT, keys, u_tabs, vT_tabs, layer):
    t = h.shape[0]
    n_tiles = N_EXPERTS // PEER_EXPERTS
    return pl.pallas_call(
        _peer_kernel,
        grid=(t // PEER_TOKENS, n_tiles),
        in_specs=[
            pl.BlockSpec((PEER_TOKENS, D_MODEL), lambda i, j: (i, 0)),
            _const_spec((PEER_HEADS * D_KEY, D_MODEL)),
            _const_spec((2, N_KEYS, D_KEY // 2)),
            pl.BlockSpec((1, PEER_EXPERTS, D_MODEL), lambda i, j: (layer, j, 0)),
            pl.BlockSpec((1, D_MODEL, PEER_EXPERTS), lambda i, j: (layer, 0, j)),
        ],
        out_specs=pl.BlockSpec((PEER_TOKENS, D_MODEL), lambda i, j: (i, 0)),
        out_shape=jax.ShapeDtypeStruct((t, D_MODEL), F32),
        scratch_shapes=[
            pltpu.VMEM((D_MODEL, PEER_TOKENS), BF16),
            pltpu.VMEM((PEER_HEADS * D_KEY, PEER_TOKENS), BF16),
            pltpu.VMEM((PEER_HEADS, N_KEYS, PEER_TOKENS), F32),
            pltpu.VMEM((PEER_HEADS, N_KEYS, PEER_TOKENS), F32),
            pltpu.VMEM((PEER_HEADS, N_KEYS, PEER_TOKENS), F32),
            pltpu.VMEM((PEER_HEADS, N_KEYS, PEER_TOKENS), F32),
            pltpu.VMEM((PEER_HEADS, N_KEYS, PEER_TOKENS), BF16),
            pltpu.VMEM((PEER_HEADS, N_KEYS, PEER_TOKENS), BF16),
            pltpu.VMEM((PEER_EXPERTS, PEER_TOKENS), BF16),
            pltpu.VMEM((PEER_EXPERTS, PEER_TOKENS), BF16),
            pltpu.VMEM((D_MODEL, PEER_TOKENS), F32),
        ],
        compiler_params=_params("arbitrary", "arbitrary"),
        name="peer_layer",
    )(h, wqT, keys, u_tabs, vT_tabs)


def _post_kernel(h_ref, y_ref, p_ref, g2_ref, b2_ref, wg_ref, bg_ref, wp_ref, o_ref):
    hn = _layer_norm(DEEPNORM_ALPHA * h_ref[...] + y_ref[...], g2_ref[...], b2_ref[...])
    gate = jax.nn.sigmoid(_dot(hn.astype(BF16), wg_ref[...]) + bg_ref[...])
    o_ref[...] = hn + gate * _dot(p_ref[0].astype(BF16), wp_ref[...])


def _post_layer(h, y, p_prompt, p_sample, layer, g2, b2, wg, bg, wp, *, split_output):
    t = h.shape[0]
    tp = p_prompt.shape[1]
    n_prompt_tiles = tp // POST_TOKENS
    tok = lambda i: (i, 0)
    prompt_tile = lambda i: jnp.minimum(i, n_prompt_tiles - 1)
    sample_tile = lambda i: jnp.maximum(i - n_prompt_tiles, 0)

    def body(h_ref, y_ref, pp_ref, ps_ref, g2_ref, b2_ref, wg_ref, bg_ref, wp_ref, *outs):
        is_prompt = pl.program_id(0) < n_prompt_tiles
        weights = (g2_ref, b2_ref, wg_ref, bg_ref, wp_ref)

        @pl.when(is_prompt)
        def _():
            _post_kernel(h_ref, y_ref, pp_ref, *weights, outs[0])

        @pl.when(jnp.logical_not(is_prompt))
        def _():
            _post_kernel(h_ref, y_ref, ps_ref, *weights, outs[-1])

    if split_output:
        out_specs = [pl.BlockSpec((POST_TOKENS, D_MODEL), lambda i: (prompt_tile(i), 0)),
                     pl.BlockSpec((POST_TOKENS, D_MODEL), lambda i: (sample_tile(i), 0))]
        out_shape = [jax.ShapeDtypeStruct((tp, D_MODEL), F32),
                     jax.ShapeDtypeStruct((t - tp, D_MODEL), F32)]
    else:
        out_specs = pl.BlockSpec((POST_TOKENS, D_MODEL), tok)
        out_shape = jax.ShapeDtypeStruct((t, D_MODEL), F32)
    return pl.pallas_call(
        body,
        grid=(t // POST_TOKENS,),
        in_specs=[
            pl.BlockSpec((POST_TOKENS, D_MODEL), tok),
            pl.BlockSpec((POST_TOKENS, D_MODEL), tok),
            pl.BlockSpec((1, POST_TOKENS, D_PLE), lambda i: (layer, prompt_tile(i), 0)),
            pl.BlockSpec((1, POST_TOKENS, D_PLE), lambda i: (layer, sample_tile(i), 0)),
            _const_spec((1, D_MODEL)),
            _const_spec((1, D_MODEL)),
            _const_spec((D_MODEL, D_MODEL)),
            _const_spec((1, D_MODEL)),
            _const_spec((D_PLE, D_MODEL)),
        ],
        out_specs=out_specs,
        out_shape=out_shape,
        compiler_params=_params("arbitrary"),
        name="post_layer",
    )(h, y, p_prompt, p_sample, g2, b2, wg, bg, wp)


def _row(v):
    return v.reshape(1, -1).astype(F32)


def _sgu_spatial_params(w_s, b_s, dec_seq):
    tri = jnp.tril(jnp.ones((CHUNK, CHUNK), bool))
    ws_prompt = jnp.where(tri, w_s, 0.0)
    tri_s = jnp.tril(jnp.ones((dec_seq, dec_seq), bool))
    small = jnp.where(tri_s, w_s[:, :dec_seq, :dec_seq], 0.0)
    eye = jnp.eye(CHUNK // dec_seq, dtype=w_s.dtype)
    ws_sample = jnp.einsum("ab,gts->gatbs", eye, small).reshape(SGU_GROUPS, CHUNK, CHUNK)
    ws2 = jnp.stack([ws_prompt, ws_sample]).astype(BF16)
    bs_prompt = b_s.T
    bs_sample = jnp.tile(b_s[:, :dec_seq].T, (CHUNK // dec_seq, 1))
    bs2 = jnp.stack([bs_prompt, bs_sample])
    bs2 = jnp.repeat(bs2, SGU_GROUP_DIM, axis=2).astype(F32)
    return ws2, bs2


def kernel(x_prompt, x_sample, cache_k_win, cache_v_win, p_prompt, p_sample,
           ln1_g, ln1_b, ln2_g, ln2_b,
           sgu_w_in, sgu_b_in, sgu_ln_g, sgu_ln_b, sgu_w_s, sgu_b_s, sgu_w_out, sgu_b_out,
           attn_w_kv, attn_w_q, attn_sinks, attn_w_o,
           peer_w_q, peer_subkeys, peer_u, peer_v,
           ple_w, ple_gate_w, ple_gate_b):
    batch, seq, _ = x_prompt.shape
    dec_batch, dec_seq, _ = x_sample.shape
    tp = batch * seq
    ts = dec_batch * dec_seq
    half = N_KV_HEADS * HEAD_DIM
    wc = cache_k_win.shape[1]
    assert wc == WINDOW and CHUNK % dec_seq == 0 and seq % WINDOW == 0
    assert dec_seq & (dec_seq - 1) == 0 and WINDOW & (WINDOW - 1) == 0
    assert tp % PEER_TOKENS == 0 and ts % PEER_TOKENS == 0 and ts % SGU_TOKENS == 0

    h = jnp.concatenate([x_prompt.reshape(tp, D_MODEL), x_sample.reshape(ts, D_MODEL)], axis=0)
    pp = p_prompt.reshape(DEPTH, tp, D_PLE)
    ps = p_sample.reshape(DEPTH, ts, D_PLE)
    ck = cache_k_win.reshape(dec_batch * wc, half).astype(F32)
    cv = cache_v_win.reshape(dec_batch * wc, half).astype(F32)
    u_tabs = peer_u.astype(BF16)
    vT_tabs = jnp.swapaxes(peer_v, 1, 2).astype(BF16)

    sgu_rows = []
    k_all = v_all = None
    for i in range(DEPTH):
        g1, b1 = _row(ln1_g[i]), _row(ln1_b[i])
        if i < N_A_LAYERS:
            ws2, bs2 = _sgu_spatial_params(sgu_w_s[i], sgu_b_s[i], dec_seq)
            h, v_rows = _sgu_layer(
                h, tp, sgu_w_in[i].astype(BF16), _row(sgu_b_in[i]), _row(sgu_ln_g[i]),
                _row(sgu_ln_b[i]), ws2, bs2, sgu_w_out[i].astype(BF16), _row(sgu_b_out[i]), g1, b1)
            sgu_rows.append(v_rows.reshape(dec_batch, dec_seq, D_SGU))
        else:
            if i == N_A_LAYERS:
                k_all, v_all = _kv_proj(h, attn_w_kv.astype(BF16))
            jl = i - N_A_LAYERS
            wqT = attn_w_q[jl].T.astype(BF16)
            woT = attn_w_o[jl].T.astype(BF16)
            sinks = attn_sinks[jl].astype(F32)
            h = _attn_layer(h, k_all, v_all, ck, cv, sinks, wqT, woT, g1, b1,
                            n_prompt_tokens=tp, blocks_per_seq=seq // WINDOW, dec_seq=dec_seq)
        y = _peer_layer(h, peer_w_q[i].T.astype(BF16), peer_subkeys[i].astype(BF16),
                        u_tabs, vT_tabs, i)
        h = _post_layer(h, y, pp, ps, i, _row(ln2_g[i]), _row(ln2_b[i]),
                        ple_gate_w[i].astype(BF16), _row(ple_gate_b[i]), ple_w[i].astype(BF16),
                        split_output=(i == DEPTH - 1))

    y_prompt = h[0].reshape(batch, seq, D_MODEL)
    y_sample = h[1].reshape(dec_batch, dec_seq, D_MODEL)
    kp4 = k_all[:tp].reshape(batch, seq, N_KV_HEADS, HEAD_DIM)
    vp4 = v_all[:tp].reshape(batch, seq, N_KV_HEADS, HEAD_DIM)
    new_k_win_prompt = kp4[:, -WINDOW:]
    new_v_win_prompt = vp4[:, -WINDOW:]
    kn4 = k_all[tp:].reshape(dec_batch, dec_seq, N_KV_HEADS, HEAD_DIM)
    vn4 = v_all[tp:].reshape(dec_batch, dec_seq, N_KV_HEADS, HEAD_DIM)
    new_k_win_sample = jnp.concatenate([cache_k_win.astype(F32), kn4], axis=1)[:, -wc:]
    new_v_win_sample = jnp.concatenate([cache_v_win.astype(F32), vn4], axis=1)[:, -wc:]
    new_sgu_v_sample = jnp.stack(sgu_rows, axis=0)
    return (y_prompt, y_sample, new_k_win_prompt, new_v_win_prompt,
            new_k_win_sample, new_v_win_sample, new_sgu_v_sample)
```

```python
import functools

import jax
import jax.numpy as jnp
from jax import lax
from jax.experimental import pallas as pl
from jax.experimental.pallas import tpu as pltpu

D_MODEL = 1024
DEPTH = 4
N_A_LAYERS = DEPTH // 2
CHUNK = 128
D_SGU = 2 * D_MODEL
SGU_GROUPS = 8
SGU_GROUP_DIM = D_SGU // SGU_GROUPS
HEAD_DIM = 64
N_HEADS = D_MODEL // HEAD_DIM
N_KV_HEADS = 2
GROUP = N_HEADS // N_KV_HEADS
WINDOW = 128
PEER_HEADS = 8
N_KEYS = 128
N_EXPERTS = N_KEYS * N_KEYS
D_KEY = 256
PEER_TOPK = 16
D_PLE = 256
DEEPNORM_ALPHA = (2.0 * DEPTH) ** 0.25
LN_EPS = 1e-5

LANES = 128
MXU_DIM = 256
VMEM_LIMIT_BYTES = 56 * 1024 * 1024

SGU_TOKENS = 256
PEER_TOKENS = 512
PEER_EXPERTS = 2048
POST_TOKENS = 512
KV_TOKENS = 512

NEG_INF = float("-inf")
NT_DIMS = (((1,), (1,)), ((), ()))

BF16 = jnp.bfloat16
F32 = jnp.float32


def _layer_norm(x, g, b):
    mu = jnp.mean(x, axis=-1, keepdims=True)
    xc = x - mu
    var = jnp.mean(xc * xc, axis=-1, keepdims=True)
    return xc * lax.rsqrt(var + LN_EPS) * g + b


def _dot(a, b):
    return jnp.dot(a, b, preferred_element_type=F32)


def _dot_nt(a, b):
    return lax.dot_general(a, b, NT_DIMS, preferred_element_type=F32)


def _const_spec(shape):
    n = len(shape)
    return pl.BlockSpec(shape, lambda *_: (0,) * n)


def _params(*semantics):
    return pltpu.CompilerParams(dimension_semantics=semantics,
                                vmem_limit_bytes=VMEM_LIMIT_BYTES)


def _sgu_kernel(x_ref, win_ref, bin_ref, lng_ref, lnb_ref, ws_ref, bs_ref,
                wout_ref, bout_ref, g1_ref, b1_ref, h_ref, v_ref,
                u_scr, v_scr):
    x = x_ref[...]
    xb = x.astype(BF16)
    n_col = (2 * D_SGU) // 512
    for j in range(n_col):
        z = _dot(xb, win_ref[:, j * 512:(j + 1) * 512]) + bin_ref[:, j * 512:(j + 1) * 512]
        z = jax.nn.gelu(z)
        if j < n_col // 2:
            u_scr[:, j * 512:(j + 1) * 512] = z
        else:
            jj = j - n_col // 2
            v_scr[:, jj * 512:(jj + 1) * 512] = z
    v = _layer_norm(v_scr[...], lng_ref[...], lnb_ref[...])
    v_ref[...] = v
    v_scr[...] = v
    for c in range(SGU_TOKENS // CHUNK):
        rows = slice(c * CHUNK, (c + 1) * CHUNK)
        for g in range(SGU_GROUPS):
            cols = slice(g * SGU_GROUP_DIM, (g + 1) * SGU_GROUP_DIM)
            mixed = _dot(ws_ref[0, g], v_scr[rows, cols].astype(BF16)) + bs_ref[0, :, cols]
            u_scr[rows, cols] = u_scr[rows, cols] * mixed
    out = _dot(u_scr[...].astype(BF16), wout_ref[...]) + bout_ref[...]
    h_ref[...] = _layer_norm(DEEPNORM_ALPHA * x + out, g1_ref[...], b1_ref[...])


def _sgu_layer(h, n_prompt_tokens, win, b_in, ln_g, ln_b, ws2, bs2, wout, b_out, g1, b1):
    t = h.shape[0]
    n_tiles = t // SGU_TOKENS
    n_prompt_tiles = n_prompt_tokens // SGU_TOKENS
    t_sample = t - n_prompt_tokens

    def kind(i):
        return jnp.where(i < n_prompt_tiles, 0, 1)

    return pl.pallas_call(
        _sgu_kernel,
        grid=(n_tiles,),
        in_specs=[
            pl.BlockSpec((SGU_TOKENS, D_MODEL), lambda i: (i, 0)),
            _const_spec((D_MODEL, 2 * D_SGU)),
            _const_spec((1, 2 * D_SGU)),
            _const_spec((1, D_SGU)),
            _const_spec((1, D_SGU)),
            pl.BlockSpec((1, SGU_GROUPS, CHUNK, CHUNK), lambda i: (kind(i), 0, 0, 0)),
            pl.BlockSpec((1, CHUNK, D_SGU), lambda i: (kind(i), 0, 0)),
            _const_spec((D_SGU, D_MODEL)),
            _const_spec((1, D_MODEL)),
            _const_spec((1, D_MODEL)),
            _const_spec((1, D_MODEL)),
        ],
        out_specs=[
            pl.BlockSpec((SGU_TOKENS, D_MODEL), lambda i: (i, 0)),
            pl.BlockSpec((SGU_TOKENS, D_SGU), lambda i: (jnp.maximum(i - n_prompt_tiles, 0), 0)),
        ],
        out_shape=[
            jax.ShapeDtypeStruct((t, D_MODEL), F32),
            jax.ShapeDtypeStruct((t_sample, D_SGU), F32),
        ],
        scratch_shapes=[
            pltpu.VMEM((SGU_TOKENS, D_SGU), F32),
            pltpu.VMEM((SGU_TOKENS, D_SGU), F32),
        ],
        compiler_params=_params("arbitrary"),
        name="sgu_layer",
    )(h, win, b_in, ln_g, ln_b, ws2, bs2, wout, b_out, g1, b1)


def _sink_softmax_rows(s, valid, sink):
    s = jnp.where(valid, s, NEG_INF)
    m = jnp.maximum(jnp.max(s, axis=0, keepdims=True), sink)
    p = jnp.exp(s - m)
    den = jnp.sum(p, axis=0, keepdims=True) + jnp.exp(sink - m)
    return p / den


def _alibi_slope(h):
    return 2.0 ** (-8.0 * (h + 1) / N_HEADS)


def _attn_prompt_kernel(sink_ref, h_ref, kp_ref, kc_ref, vp_ref, vc_ref, wqT_ref, woT_ref,
                        g1_ref, b1_ref, o_ref, oT_scr, *, blocks_per_seq):
    hb = h_ref[...]
    qT = _dot_nt(wqT_ref[...], hb.astype(BF16)).astype(BF16)
    kk = jnp.concatenate([kp_ref[...], kc_ref[...]], axis=0).astype(BF16)
    vv = jnp.concatenate([vp_ref[...], vc_ref[...]], axis=0)
    vvT = vv.T.astype(BF16)

    kj = lax.broadcasted_iota(jnp.int32, (2 * WINDOW, WINDOW), 0)
    qi = lax.broadcasted_iota(jnp.int32, (2 * WINDOW, WINDOW), 1)
    dist = qi + WINDOW - kj
    is_first = (pl.program_id(0) % blocks_per_seq == 0).astype(jnp.int32)
    valid = (dist >= 0) & (dist < WINDOW) & (kj >= is_first * WINDOW)
    distf = dist.astype(F32)

    zeros_q = jnp.zeros((HEAD_DIM, GROUP * WINDOW), BF16)
    for k in range(N_KV_HEADS):
        heads = range(k * GROUP, (k + 1) * GROUP)
        q_grp = jnp.concatenate([qT[h * HEAD_DIM:(h + 1) * HEAD_DIM] for h in heads], axis=1)
        q_pad = jnp.concatenate([q_grp, zeros_q] if k == 0 else [zeros_q, q_grp], axis=0)
        s_all = _dot(kk, q_pad) * (HEAD_DIM ** -0.5)
        p_blocks = []
        for g, h in enumerate(heads):
            s = s_all[:, g * WINDOW:(g + 1) * WINDOW] - _alibi_slope(h) * distf
            p_blocks.append(_sink_softmax_rows(s, valid, sink_ref[h]).astype(BF16))
        o_all = _dot(vvT[k * HEAD_DIM:(k + 1) * HEAD_DIM],
                     jnp.concatenate(p_blocks, axis=1))
        for g, h in enumerate(heads):
            oT_scr[h * HEAD_DIM:(h + 1) * HEAD_DIM, :] = o_all[:, g * WINDOW:(g + 1) * WINDOW]
    outT = _dot(woT_ref[...], oT_scr[...].astype(BF16))
    o_ref[...] = _layer_norm(DEEPNORM_ALPHA * hb + outT.T, g1_ref[...], b1_ref[...])


def _attn_sample_kernel(sink_ref, h_ref, ck_ref, kn_ref, cv_ref, vn_ref, wqT_ref, woT_ref,
                        g1_ref, b1_ref, o_ref, oT_scr, *, dec_seq):
    n_seq = WINDOW // dec_seq
    n_cache = n_seq * WINDOW
    n_keys = n_cache + WINDOW
    hb = h_ref[...]
    qT = _dot_nt(wqT_ref[...], hb.astype(BF16)).astype(BF16)
    kk = jnp.concatenate([ck_ref[...], kn_ref[...]], axis=0).astype(BF16)
    vv = jnp.concatenate([cv_ref[...], vn_ref[...]], axis=0)
    vvT = vv.T.astype(BF16)

    kidx = lax.broadcasted_iota(jnp.int32, (n_keys, WINDOW), 0)
    q = lax.broadcasted_iota(jnp.int32, (n_keys, WINDOW), 1)
    seq_shift, win_shift = dec_seq.bit_length() - 1, WINDOW.bit_length() - 1
    q_seq, q_pos = q >> seq_shift, q & (dec_seq - 1)
    is_cache = kidx < n_cache
    new_idx = jnp.maximum(kidx - n_cache, 0)
    k_seq = jnp.where(is_cache, kidx >> win_shift, new_idx >> seq_shift)
    dist = jnp.where(is_cache, q_pos + WINDOW - (kidx & (WINDOW - 1)),
                     q_pos - (new_idx & (dec_seq - 1)))
    valid = (k_seq == q_seq) & (dist >= 0) & (dist < WINDOW)
    distf = dist.astype(F32)

    zeros_q = jnp.zeros((HEAD_DIM, WINDOW), BF16)
    for h in range(N_HEADS):
        k = h // GROUP
        q_h = qT[h * HEAD_DIM:(h + 1) * HEAD_DIM]
        q_pad = jnp.concatenate([q_h, zeros_q] if k == 0 else [zeros_q, q_h], axis=0)
        s = _dot(kk, q_pad) * (HEAD_DIM ** -0.5) - _alibi_slope(h) * distf
        p = _sink_softmax_rows(s, valid, sink_ref[h]).astype(BF16)
        oT_scr[h * HEAD_DIM:(h + 1) * HEAD_DIM, :] = _dot(vvT[k * HEAD_DIM:(k + 1) * HEAD_DIM], p)
    outT = _dot(woT_ref[...], oT_scr[...].astype(BF16))
    o_ref[...] = _layer_norm(DEEPNORM_ALPHA * hb + outT.T, g1_ref[...], b1_ref[...])


def _attn_layer(h, k_all, v_all, cache_k, cache_v, sinks, wqT, woT, g1, b1,
                *, n_prompt_tokens, blocks_per_seq, dec_seq):
    t = h.shape[0]
    half = N_KV_HEADS * HEAD_DIM
    n_prompt_blocks = n_prompt_tokens // WINDOW
    n_sample_blocks = (t - n_prompt_tokens) // WINDOW
    n_seq = WINDOW // dec_seq
    cur_map = lambda i: (i, 0)
    prev_map = lambda i: (jnp.maximum(i - 1, 0), 0)
    cache_map = lambda i: (jnp.clip(i - n_prompt_blocks, 0, n_sample_blocks - 1), 0)

    def body(sink_ref, h_ref, kp_ref, kc_ref, vp_ref, vc_ref, ck_ref, cv_ref, *rest):
        is_prompt = pl.program_id(0) < n_prompt_blocks

        @pl.when(is_prompt)
        def _():
            _attn_prompt_kernel(sink_ref, h_ref, kp_ref, kc_ref, vp_ref, vc_ref, *rest,
                                blocks_per_seq=blocks_per_seq)

        @pl.when(jnp.logical_not(is_prompt))
        def _():
            _attn_sample_kernel(sink_ref, h_ref, ck_ref, kc_ref, cv_ref, vc_ref, *rest,
                                dec_seq=dec_seq)

    return pl.pallas_call(
        body,
        grid=(n_prompt_blocks + n_sample_blocks,),
        in_specs=[
            pl.BlockSpec(memory_space=pltpu.SMEM),
            pl.BlockSpec((WINDOW, D_MODEL), cur_map),
            pl.BlockSpec((WINDOW, half), prev_map),
            pl.BlockSpec((WINDOW, half), cur_map),
            pl.BlockSpec((WINDOW, half), prev_map),
            pl.BlockSpec((WINDOW, half), cur_map),
            pl.BlockSpec((n_seq * WINDOW, half), cache_map),
            pl.BlockSpec((n_seq * WINDOW, half), cache_map),
            _const_spec((D_MODEL, D_MODEL)),
            _const_spec((D_MODEL, D_MODEL)),
            _const_spec((1, D_MODEL)),
            _const_spec((1, D_MODEL)),
        ],
        out_specs=pl.BlockSpec((WINDOW, D_MODEL), cur_map),
        out_shape=jax.ShapeDtypeStruct((t, D_MODEL), F32),
        scratch_shapes=[pltpu.VMEM((D_MODEL, WINDOW), F32)],
        compiler_params=_params("arbitrary"),
        name="attn_layer",
    )(sinks, h, k_all, k_all, v_all, v_all, cache_k, cache_v, wqT, woT, g1, b1)


NOT_RANKED = 127.0


def _top16_rows(s):
    n = N_KEYS // 8
    v = [s[8 * i:8 * (i + 1)] for i in range(n)]
    k = 2
    while k <= n:
        j = k // 2
        while j >= 1:
            for i in range(n):
                l = i ^ j
                if l > i:
                    hi, lo = jnp.maximum(v[i], v[l]), jnp.minimum(v[i], v[l])
                    v[i], v[l] = (hi, lo) if (i & k) == 0 else (lo, hi)
            j //= 2
        k *= 2
    row_id = lax.broadcasted_iota(jnp.int32, (PEER_TOPK, LANES), 0)
    rows = []
    stacked = jnp.full((PEER_TOPK, LANES), NEG_INF, F32)
    for i in range(PEER_TOPK):
        m = jnp.max(v[0], axis=0, keepdims=True)
        rows.append(m)
        stacked = jnp.where(row_id == i, m, stacked)
        hit = v[0] == m
        for d in range(PEER_TOPK - 1 - i):
            v[d] = jnp.where(hit, v[d + 1], v[d])
    return rows, stacked


def _rank_of(s, rows):
    rank = jnp.full(s.shape, NOT_RANKED, F32)
    for i in reversed(range(len(rows))):
        rank = jnp.where(s >= rows[i], float(i), rank)
    return rank


def _pair_threshold(r1, t1, r2, t2):
    row8 = lax.broadcasted_iota(jnp.int32, (8, LANES), 0)
    cands = []
    for i in range(8):
        n = PEER_TOPK // (i + 1)
        c = r1[i] + t2[0:8]
        if n < 8:
            c = jnp.where(row8 < n, c, NEG_INF)
        cands.append(c)
    cands.append(r1[0] + t2[8:16])
    cands.append(t1[8:16] + r2[0])
    top = r1[0] + r2[0]
    z = jnp.zeros((1, LANES), F32)
    m = top
    for _ in range(PEER_TOPK):
        m = cands[0]
        for c in cands[1:]:
            m = jnp.maximum(m, c)
        m = jnp.max(m, axis=0, keepdims=True)
        z = z + jnp.exp(m - top)
        cands = [jnp.where(c == m, NEG_INF, c) for c in cands]
    return m, z


def _peer_select(x_ref, wqT_ref, keys_ref, xb_scr, qT_scr, s1_scr, s2_scr,
                 cnt_scr, e1_scr, rank2_scr, e2_scr):
    xb_scr[...] = x_ref[...].T.astype(BF16)
    qT_scr[...] = _dot(wqT_ref[...], xb_scr[...]).astype(BF16)
    for h in range(PEER_HEADS):
        for p, dst in ((0, s1_scr), (1, s2_scr)):
            r0 = (h * 2 + p) * (D_KEY // 2)
            dst[h] = _dot(keys_ref[p], qT_scr[r0:r0 + D_KEY // 2, :])

    def chunk_body(c, carry):
        lanes = pl.ds(pl.multiple_of(c * LANES, LANES), LANES)

        def one_head(h):
            s1 = s1_scr[h, :, lanes]
            s2 = s2_scr[h, :, lanes]
            r1, t1 = _top16_rows(s1)
            r2, t2 = _top16_rows(s2)
            tau, z = _pair_threshold(r1, t1, r2, t2)
            cnt_by_rank = jnp.zeros(t1.shape, F32)
            for jx in range(PEER_TOPK):
                cnt_by_rank = cnt_by_rank + jnp.where(t1 + r2[jx] >= tau, 1.0, 0.0)
            cnt = jnp.zeros(s1.shape, F32)
            for i in reversed(range(PEER_TOPK)):
                cnt = jnp.where(s1 >= r1[i], cnt_by_rank[i:i + 1], cnt)
            cnt_scr[h, :, lanes] = cnt
            e1_scr[h, :, lanes] = jnp.exp(s1 - r1[0]) * (1.0 / z)
            rank2_scr[h, :, lanes] = _rank_of(s2, r2).astype(BF16)
            e2_scr[h, :, lanes] = jnp.exp(s2 - r2[0]).astype(BF16)

        def head_pair_body(hp, carry2):
            one_head(2 * hp)
            one_head(2 * hp + 1)
            return carry2

        return lax.fori_loop(0, PEER_HEADS // 2, head_pair_body, carry)

    lax.fori_loop(0, PEER_TOKENS // LANES, chunk_body, 0)


def _peer_gate(j, cnt_scr, e1_scr, rank2_scr, e2_scr):
    a_per_step = PEER_EXPERTS // N_KEYS
    a_rows = pl.ds(pl.multiple_of(j * a_per_step, a_per_step), a_per_step)
    zero = jnp.zeros((), BF16)
    out = [[None] * (PEER_TOKENS // LANES) for _ in range(a_per_step)]
    for c in range(PEER_TOKENS // LANES):
        lanes = slice(c * LANES, (c + 1) * LANES)
        cnt_blk = [cnt_scr[h, a_rows, lanes] for h in range(PEER_HEADS)]
        e1_blk = [e1_scr[h, a_rows, lanes] for h in range(PEER_HEADS)]
        for al in range(a_per_step):
            w = None
            for h in range(PEER_HEADS):
                cnt = jnp.broadcast_to(cnt_blk[h][al:al + 1], (N_KEYS, LANES)).astype(BF16)
                e1 = jnp.broadcast_to(e1_blk[h][al:al + 1], (N_KEYS, LANES)).astype(BF16)
                wh = jnp.where(rank2_scr[h, :, lanes] < cnt, e2_scr[h, :, lanes] * e1, zero)
                w = wh if w is None else w + wh
            out[al][c] = w
    return [jnp.concatenate(row, axis=1) for row in out]


def _peer_kernel(x_ref, wqT_ref, keys_ref, u_ref, vT_ref, o_ref,
                 xb_scr, qT_scr, s1_scr, s2_scr, cnt_scr, e1_scr, rank2_scr, e2_scr,
                 hid_scr, acc_scr):
    j = pl.program_id(1)

    @pl.when(j == 0)
    def _start():
        _peer_select(x_ref, wqT_ref, keys_ref, xb_scr, qT_scr, s1_scr, s2_scr,
                     cnt_scr, e1_scr, rank2_scr, e2_scr)
        acc_scr[...] = jnp.zeros_like(acc_scr)

    hid_scr[...] = _dot(u_ref[0], xb_scr[...]).astype(BF16)
    w_rows = _peer_gate(j, cnt_scr, e1_scr, rank2_scr, e2_scr)
    total = None
    per_chunk = MXU_DIM // N_KEYS
    for k in range(PEER_EXPERTS // MXU_DIM):
        rows = slice(k * MXU_DIM, (k + 1) * MXU_DIM)
        w = jnp.concatenate(w_rows[k * per_chunk:(k + 1) * per_chunk], axis=0)
        g = jax.nn.gelu(hid_scr[rows, :]) * w
        part = _dot(vT_ref[0, :, rows], g)
        total = part if total is None else total + part
    acc_scr[...] += total

    @pl.when(j == pl.num_programs(1) - 1)
    def _finish():
        o_ref[...] = acc_scr[...].T


def _peer_layer(h, wqT, keys, u_tabs, vT_tabs, layer):
    t = h.shape[0]
    n_tiles = N_EXPERTS // PEER_EXPERTS
    return pl.pallas_call(
        _peer_kernel,
        grid=(t // PEER_TOKENS, n_tiles),
        in_specs=[
            pl.BlockSpec((PEER_TOKENS, D_MODEL), lambda i, j: (i, 0)),
            _const_spec((PEER_HEADS * D_KEY, D_MODEL)),
            _const_spec((2, N_KEYS, D_KEY // 2)),
            pl.BlockSpec((1, PEER_EXPERTS, D_MODEL), lambda i, j: (layer, j, 0)),
            pl.BlockSpec((1, D_MODEL, PEER_EXPERTS), lambda i, j: (layer, 0, j)),
        ],
        out_specs=pl.BlockSpec((PEER_TOKENS, D_MODEL), lambda i, j: (i, 0)),
        out_shape=jax.ShapeDtypeStruct((t, D_MODEL), F32),
        scratch_shapes=[
            pltpu.VMEM((D_MODEL, PEER_TOKENS), BF16),
            pltpu.VMEM((PEER_HEADS * D_KEY, PEER_TOKENS), BF16),
            pltpu.VMEM((PEER_HEADS, N_KEYS, PEER_TOKENS), F32),
            pltpu.VMEM((PEER_HEADS, N_KEYS, PEER_TOKENS), F32),
            pltpu.VMEM((PEER_HEADS, N_KEYS, PEER_TOKENS), F32),
            pltpu.VMEM((PEER_HEADS, N_KEYS, PEER_TOKENS), F32),
            pltpu.VMEM((PEER_HEADS, N_KEYS, PEER_TOKENS), BF16),
            pltpu.VMEM((PEER_HEADS, N_KEYS, PEER_TOKENS), BF16),
            pltpu.VMEM((PEER_EXPERTS, PEER_TOKENS), BF16),
            pltpu.VMEM((D_MODEL, PEER_TOKENS), F32),
        ],
        compiler_params=_params("arbitrary", "arbitrary"),
        name="peer_layer",
    )(h, wqT, keys, u_tabs, vT_tabs)


def _post_kernel(h_ref, y_ref, p_ref, g2_ref, b2_ref, wg_ref, bg_ref, wp_ref, o_ref):
    hn = _layer_norm(DEEPNORM_ALPHA * h_ref[...] + y_ref[...], g2_ref[...], b2_ref[...])
    gate = jax.nn.sigmoid(_dot(hn.astype(BF16), wg_ref[...]) + bg_ref[...])
    o_ref[...] = hn + gate * _dot(p_ref[0].astype(BF16), wp_ref[...])


def _post_layer(h, y, p_prompt, p_sample, layer, g2, b2, wg, bg, wp, *, split_output, w_kv=None):
    t = h.shape[0]
    tp = p_prompt.shape[1]
    half = N_KV_HEADS * HEAD_DIM
    n_prompt_tiles = tp // POST_TOKENS
    tok = lambda i: (i, 0)
    prompt_tile = lambda i: jnp.minimum(i, n_prompt_tiles - 1)
    sample_tile = lambda i: jnp.maximum(i - n_prompt_tiles, 0)
    assert not (split_output and w_kv is not None)

    def body(h_ref, y_ref, pp_ref, ps_ref, g2_ref, b2_ref, wg_ref, bg_ref, wp_ref, *rest):
        is_prompt = pl.program_id(0) < n_prompt_tiles
        weights = (g2_ref, b2_ref, wg_ref, bg_ref, wp_ref)
        outs = rest if w_kv is None else rest[1:2]

        @pl.when(is_prompt)
        def _():
            _post_kernel(h_ref, y_ref, pp_ref, *weights, outs[0])

        @pl.when(jnp.logical_not(is_prompt))
        def _():
            _post_kernel(h_ref, y_ref, ps_ref, *weights, outs[-1])

        if w_kv is not None:
            wkv_ref, o_ref, k_ref, v_ref = rest
            kv = _dot(o_ref[...].astype(BF16), wkv_ref[...])
            k_ref[...] = kv[:, :half]
            v_ref[...] = kv[:, half:]

    extra_in = [] if w_kv is None else [_const_spec((D_MODEL, 2 * half))]
    extra_args = () if w_kv is None else (w_kv,)
    if w_kv is not None:
        out_specs = [pl.BlockSpec((POST_TOKENS, D_MODEL), tok),
                     pl.BlockSpec((POST_TOKENS, half), tok),
                     pl.BlockSpec((POST_TOKENS, half), tok)]
        out_shape = [jax.ShapeDtypeStruct((t, D_MODEL), F32),
                     jax.ShapeDtypeStruct((t, half), F32),
                     jax.ShapeDtypeStruct((t, half), F32)]
    elif split_output:
        out_specs = [pl.BlockSpec((POST_TOKENS, D_MODEL), lambda i: (prompt_tile(i), 0)),
                     pl.BlockSpec((POST_TOKENS, D_MODEL), lambda i: (sample_tile(i), 0))]
        out_shape = [jax.ShapeDtypeStruct((tp, D_MODEL), F32),
                     jax.ShapeDtypeStruct((t - tp, D_MODEL), F32)]
    else:
        out_specs = pl.BlockSpec((POST_TOKENS, D_MODEL), tok)
        out_shape = jax.ShapeDtypeStruct((t, D_MODEL), F32)
    return pl.pallas_call(
        body,
        grid=(t // POST_TOKENS,),
        in_specs=[
            pl.BlockSpec((POST_TOKENS, D_MODEL), tok),
            pl.BlockSpec((POST_TOKENS, D_MODEL), tok),
            pl.BlockSpec((1, POST_TOKENS, D_PLE), lambda i: (layer, prompt_tile(i), 0)),
            pl.BlockSpec((1, POST_TOKENS, D_PLE), lambda i: (layer, sample_tile(i), 0)),
            _const_spec((1, D_MODEL)),
            _const_spec((1, D_MODEL)),
            _const_spec((D_MODEL, D_MODEL)),
            _const_spec((1, D_MODEL)),
            _const_spec((D_PLE, D_MODEL)),
        ] + extra_in,
        out_specs=out_specs,
        out_shape=out_shape,
        compiler_params=_params("arbitrary"),
        name="post_layer",
    )(h, y, p_prompt, p_sample, g2, b2, wg, bg, wp, *extra_args)


def _row(v):
    return v.reshape(1, -1).astype(F32)


def _sgu_spatial_params(w_s, b_s, dec_seq):
    tri = jnp.tril(jnp.ones((CHUNK, CHUNK), bool))
    ws_prompt = jnp.where(tri, w_s, 0.0)
    tri_s = jnp.tril(jnp.ones((dec_seq, dec_seq), bool))
    small = jnp.where(tri_s, w_s[:, :dec_seq, :dec_seq], 0.0)
    eye = jnp.eye(CHUNK // dec_seq, dtype=w_s.dtype)
    ws_sample = jnp.einsum("ab,gts->gatbs", eye, small).reshape(SGU_GROUPS, CHUNK, CHUNK)
    ws2 = jnp.stack([ws_prompt, ws_sample]).astype(BF16)
    bs_prompt = b_s.T
    bs_sample = jnp.tile(b_s[:, :dec_seq].T, (CHUNK // dec_seq, 1))
    bs2 = jnp.stack([bs_prompt, bs_sample])
    bs2 = jnp.repeat(bs2, SGU_GROUP_DIM, axis=2).astype(F32)
    return ws2, bs2


def kernel(x_prompt, x_sample, cache_k_win, cache_v_win, p_prompt, p_sample,
           ln1_g, ln1_b, ln2_g, ln2_b,
           sgu_w_in, sgu_b_in, sgu_ln_g, sgu_ln_b, sgu_w_s, sgu_b_s, sgu_w_out, sgu_b_out,
           attn_w_kv, attn_w_q, attn_sinks, attn_w_o,
           peer_w_q, peer_subkeys, peer_u, peer_v,
           ple_w, ple_gate_w, ple_gate_b):
    batch, seq, _ = x_prompt.shape
    dec_batch, dec_seq, _ = x_sample.shape
    tp = batch * seq
    ts = dec_batch * dec_seq
    half = N_KV_HEADS * HEAD_DIM
    wc = cache_k_win.shape[1]
    assert wc == WINDOW and CHUNK % dec_seq == 0 and seq % WINDOW == 0
    assert dec_seq & (dec_seq - 1) == 0 and WINDOW & (WINDOW - 1) == 0
    assert tp % PEER_TOKENS == 0 and ts % PEER_TOKENS == 0 and ts % SGU_TOKENS == 0

    h = jnp.concatenate([x_prompt.reshape(tp, D_MODEL), x_sample.reshape(ts, D_MODEL)], axis=0)
    pp = p_prompt.reshape(DEPTH, tp, D_PLE)
    ps = p_sample.reshape(DEPTH, ts, D_PLE)
    ck = cache_k_win.reshape(dec_batch * wc, half).astype(F32)
    cv = cache_v_win.reshape(dec_batch * wc, half).astype(F32)
    u_tabs = peer_u.astype(BF16)
    vT_tabs = jnp.swapaxes(peer_v, 1, 2).astype(BF16)

    sgu_rows = []
    k_all = v_all = None
    for i in range(DEPTH):
        g1, b1 = _row(ln1_g[i]), _row(ln1_b[i])
        if i < N_A_LAYERS:
            ws2, bs2 = _sgu_spatial_params(sgu_w_s[i], sgu_b_s[i], dec_seq)
            h, v_rows = _sgu_layer(
                h, tp, sgu_w_in[i].astype(BF16), _row(sgu_b_in[i]), _row(sgu_ln_g[i]),
                _row(sgu_ln_b[i]), ws2, bs2, sgu_w_out[i].astype(BF16), _row(sgu_b_out[i]), g1, b1)
            sgu_rows.append(v_rows.reshape(dec_batch, dec_seq, D_SGU))
        else:
            jl = i - N_A_LAYERS
            wqT = attn_w_q[jl].T.astype(BF16)
            woT = attn_w_o[jl].T.astype(BF16)
            sinks = attn_sinks[jl].astype(F32)
            h = _attn_layer(h, k_all, v_all, ck, cv, sinks, wqT, woT, g1, b1,
                            n_prompt_tokens=tp, blocks_per_seq=seq // WINDOW, dec_seq=dec_seq)
        y = _peer_layer(h, peer_w_q[i].T.astype(BF16), peer_subkeys[i].astype(BF16),
                        u_tabs, vT_tabs, i)
        feeds_kv = i == N_A_LAYERS - 1
        h = _post_layer(h, y, pp, ps, i, _row(ln2_g[i]), _row(ln2_b[i]),
                        ple_gate_w[i].astype(BF16), _row(ple_gate_b[i]), ple_w[i].astype(BF16),
                        split_output=(i == DEPTH - 1),
                        w_kv=attn_w_kv.astype(BF16) if feeds_kv else None)
        if feeds_kv:
            h, k_all, v_all = h

    y_prompt = h[0].reshape(batch, seq, D_MODEL)
    y_sample = h[1].reshape(dec_batch, dec_seq, D_MODEL)
    kp4 = k_all[:tp].reshape(batch, seq, N_KV_HEADS, HEAD_DIM)
    vp4 = v_all[:tp].reshape(batch, seq, N_KV_HEADS, HEAD_DIM)
    new_k_win_prompt = kp4[:, -WINDOW:]
    new_v_win_prompt = vp4[:, -WINDOW:]
    kn4 = k_all[tp:].reshape(dec_batch, dec_seq, N_KV_HEADS, HEAD_DIM)
    vn4 = v_all[tp:].reshape(dec_batch, dec_seq, N_KV_HEADS, HEAD_DIM)
    new_k_win_sample = jnp.concatenate([cache_k_win.astype(F32), kn4], axis=1)[:, -wc:]
    new_v_win_sample = jnp.concatenate([cache_v_win.astype(F32), vn4], axis=1)[:, -wc:]
    new_sgu_v_sample = jnp.stack(sgu_rows, axis=0)
    return (y_prompt, y_sample, new_k_win_prompt, new_v_win_prompt,
            new_k_win_sample, new_v_win_sample, new_sgu_v_sample)
```
